```python
import math
import jax, jax.numpy as jnp
from jax import lax
import numpy as np

D_MODEL = 1024
BATCH = 8
SEQ = 4096
DEPTH = 1

GRID_W = 64
CTX_LEN = 256
N_HEADS = 8
HEAD_DIM = 64
V_DIM = 2 * HEAD_DIM
ATTN_WIDTH = N_HEADS * V_DIM
QK_WIDTH = N_HEADS * 2 * HEAD_DIM
D_CONV = D_MODEL
CONV_WIDTH = 31
N_EXPERTS = 16
CAPACITY_FACTOR = 2
D_FF_EXPERT = 2816
ROPE_THETA = 10000.0
Q_BLOCK = 128
EPS = 1e-6
N_MOD = 6
PROJ_WIDTH = 2 * D_CONV + 2 * QK_WIDTH + ATTN_WIDTH + 2 * D_MODEL

kernel_name = "hybrid_conv_diffattn_ecmoe_dit_layer"


def rmsnorm(x, g):
    xf = x.astype(jnp.float32)
    y = xf * lax.rsqrt(jnp.mean(xf * xf, axis=-1, keepdims=True) + EPS)
    return (y * g.astype(jnp.float32)).astype(x.dtype)


def layernorm(x, g, b):
    xf = x.astype(jnp.float32)
    mu = jnp.mean(xf, axis=-1, keepdims=True)
    xc = xf - mu
    y = xc * lax.rsqrt(jnp.mean(xc * xc, axis=-1, keepdims=True) + EPS)
    return (y * g.astype(jnp.float32) + b.astype(jnp.float32)).astype(x.dtype)


def modulate(h, shift, scale):
    return h * (1 + scale) + shift


def axial_rope_tables(n_tokens):
    rows = n_tokens // GRID_W
    row = jnp.repeat(jnp.arange(rows), GRID_W)
    col = jnp.tile(jnp.arange(GRID_W), rows)
    n_freq = HEAD_DIM // 4
    inv = ROPE_THETA ** (-jnp.arange(n_freq, dtype=jnp.float32) / n_freq)
    ang_r = row[:, None].astype(jnp.float32) * inv
    ang_c = col[:, None].astype(jnp.float32) * inv
    return jnp.cos(ang_r), jnp.sin(ang_r), jnp.cos(ang_c), jnp.sin(ang_c)


def rope_1d(x, cos, sin):
    x1, x2 = jnp.split(x, 2, axis=-1)
    return jnp.concatenate([x1 * cos - x2 * sin, x1 * sin + x2 * cos], axis=-1)


def apply_axial_rope(x, tables):
    cos_r, sin_r, cos_c, sin_c = [t[:, None, None, :].astype(x.dtype) for t in tables]
    xr, xc = jnp.split(x, 2, axis=-1)
    return jnp.concatenate([rope_1d(xr, cos_r, sin_r), rope_1d(xc, cos_c, sin_c)], axis=-1)


def split_projection(p):
    sizes = [D_CONV, D_CONV, QK_WIDTH, QK_WIDTH, ATTN_WIDTH, D_MODEL, D_MODEL]
    cuts = list(np.cumsum(sizes)[:-1])
    return jnp.split(p, cuts, axis=-1)


def conv_branch(a, b, w_dw, b_dw, ln_g, ln_b, w_o, b_o):
    u = a * jax.nn.sigmoid(b)
    u = lax.conv_general_dilated(
        u, w_dw[:, None, :].astype(u.dtype), window_strides=(1,),
        padding=[(CONV_WIDTH // 2, CONV_WIDTH // 2)],
        dimension_numbers=("NWC", "WIO", "NWC"), feature_group_count=D_CONV) + b_dw
    u = jax.nn.silu(layernorm(u, ln_g, ln_b))
    return u @ w_o + b_o


def diff_weights(q, k, lam):
    logits = jnp.einsum("bqhmd,bkhmd->bhmqk", q, k).astype(jnp.float32) * (HEAD_DIM ** -0.5)
    p = jax.nn.softmax(logits, axis=-1)
    return p[:, :, 0] - lam * p[:, :, 1]


def diff_attention_blocks(q, k, v, lam):
    bsz, t = q.shape[0], q.shape[1]
    n_blk = t // Q_BLOCK
    qb = q.reshape(bsz, n_blk, Q_BLOCK, N_HEADS, 2, HEAD_DIM).swapaxes(0, 1)

    def one_block(q_blk):
        w = diff_weights(q_blk, k, lam)
        return jnp.einsum("bhqk,bkhe->bqhe", w.astype(v.dtype), v)

    out = lax.map(one_block, qb)
    return out.swapaxes(0, 1).reshape(bsz, t, N_HEADS, V_DIM)


def diff_head_out(o, g_subln, lam_init, w_o):
    bsz, t = o.shape[0], o.shape[1]
    o = rmsnorm(o, g_subln) * (1.0 - lam_init)
    return o.reshape(bsz, t, ATTN_WIDTH) @ w_o


def ec_moe(h, w_router, w_g, w_u, w_d):
    bsz, t = h.shape[0], h.shape[1]
    cap = CAPACITY_FACTOR * t // N_EXPERTS
    aff = jax.nn.softmax(jnp.einsum("btd,de->bte", h, w_router).astype(jnp.float32), axis=-1)
    gates, idx = lax.top_k(aff.transpose(0, 2, 1), cap)
    bidx = jnp.arange(bsz)[:, None, None]
    xs = h[bidx, idx]
    hid = jax.nn.silu(jnp.einsum("becd,edf->becf", xs, w_g)) * jnp.einsum("becd,edf->becf", xs, w_u)
    ys = jnp.einsum("becf,efd->becd", hid, w_d) * gates[..., None].astype(h.dtype)
    return jnp.zeros_like(h).at[bidx, idx].add(ys)


def setup_inputs(seed: int = 0) -> dict:
    key = jax.random.key(seed)
    ks = jax.random.split(key, 32)
    f32 = jnp.float32

    def nrm(k, shape, scale):
        return jax.random.normal(k, shape, f32) * scale

    L = DEPTH
    return {
        "x": nrm(ks[0], (BATCH, SEQ, D_MODEL), 1.0),
        "c": nrm(ks[1], (BATCH, D_MODEL), 1.0),
        "ctx": nrm(ks[2], (BATCH, CTX_LEN, D_MODEL), 1.0),
        "c_ctx": nrm(ks[3], (D_MODEL,), 1.0),
        "w_ada": nrm(ks[4], (L, D_MODEL, N_MOD * D_MODEL), 0.5 * D_MODEL ** -0.5),
        "b_ada": nrm(ks[5], (L, N_MOD * D_MODEL), 0.02),
        "g_norm_mix": 1.0 + nrm(ks[6], (L, D_MODEL), 0.02),
        "g_norm_ffn": 1.0 + nrm(ks[7], (L, D_MODEL), 0.02),
        "w_in": nrm(ks[8], (L, D_MODEL, PROJ_WIDTH), D_MODEL ** -0.5),
        "w_dw": nrm(ks[9], (L, CONV_WIDTH, D_CONV), CONV_WIDTH ** -0.5),
        "b_dw": nrm(ks[10], (L, D_CONV), 0.02),
        "ln_g_conv": 1.0 + nrm(ks[11], (L, D_CONV), 0.02),
        "ln_b_conv": nrm(ks[12], (L, D_CONV), 0.02),
        "w_conv_out": nrm(ks[13], (L, D_CONV, D_MODEL), D_CONV ** -0.5),
        "b_conv_out": nrm(ks[14], (L, D_MODEL), 0.02),
        "lambda_q1": nrm(ks[15], (L, HEAD_DIM), 0.1),
        "lambda_k1": nrm(ks[16], (L, HEAD_DIM), 0.1),
        "lambda_q2": nrm(ks[17], (L, HEAD_DIM), 0.1),
        "lambda_k2": nrm(ks[18], (L, HEAD_DIM), 0.1),
        "g_subln": 1.0 + nrm(ks[19], (L, V_DIM), 0.02),
        "w_attn_out": nrm(ks[20], (L, ATTN_WIDTH, D_MODEL), ATTN_WIDTH ** -0.5),
        "w_out": nrm(ks[21], (L, D_MODEL, D_MODEL), D_MODEL ** -0.5),
        "w_router": nrm(ks[22], (L, D_MODEL, N_EXPERTS), D_MODEL ** -0.5),
        "w_expert_gate": nrm(ks[23], (L, N_EXPERTS, D_MODEL, D_FF_EXPERT), D_MODEL ** -0.5),
        "w_expert_up": nrm(ks[24], (L, N_EXPERTS, D_MODEL, D_FF_EXPERT), D_MODEL ** -0.5),
        "w_expert_down": nrm(ks[25], (L, N_EXPERTS, D_FF_EXPERT, D_MODEL), D_FF_EXPERT ** -0.5),
        "g_final": 1.0 + nrm(ks[26], (D_MODEL,), 0.02),
    }


def reference(x, c, ctx, c_ctx, w_ada, b_ada, g_norm_mix, g_norm_ffn, w_in, w_dw, b_dw,
              ln_g_conv, ln_b_conv, w_conv_out, b_conv_out, lambda_q1, lambda_k1, lambda_q2,
              lambda_k2, g_subln, w_attn_out, w_out, w_router, w_expert_gate, w_expert_up,
              w_expert_down, g_final):
    bsz, t = x.shape[0], x.shape[1]
    n_ctx = ctx.shape[1]
    rope = axial_rope_tables(t)

    for l in range(DEPTH):
        last = l == DEPTH - 1
        lam_init = 0.8 - 0.6 * math.exp(-0.3 * l)
        lam = (jnp.exp(jnp.sum(lambda_q1[l].astype(jnp.float32) * lambda_k1[l].astype(jnp.float32)))
               - jnp.exp(jnp.sum(lambda_q2[l].astype(jnp.float32) * lambda_k2[l].astype(jnp.float32)))
               + lam_init)

        mod_lat = (jax.nn.silu(c) @ w_ada[l] + b_ada[l])[:, None, :]
        mod_ctx = jax.nn.silu(c_ctx) @ w_ada[l] + b_ada[l]
        sh_m, sc_m, g_m, sh_f, sc_f, g_f = jnp.split(mod_lat, N_MOD, axis=-1)
        csh_m, csc_m, cg_m, csh_f, csc_f, cg_f = jnp.split(mod_ctx, N_MOD, axis=-1)

        h_lat = modulate(rmsnorm(x, g_norm_mix[l]), sh_m, sc_m)
        h_ctx = modulate(rmsnorm(ctx, g_norm_mix[l]), csh_m, csc_m)
        a_l, b_l, q_l, k_l, v_l, gc_l, ga_l = split_projection(h_lat @ w_in[l])
        a_c, b_c, q_c, k_c, v_c, gc_c, ga_c = split_projection(h_ctx @ w_in[l])

        q_l = apply_axial_rope(q_l.reshape(bsz, t, N_HEADS, 2, HEAD_DIM), rope)
        k_l = apply_axial_rope(k_l.reshape(bsz, t, N_HEADS, 2, HEAD_DIM), rope)
        v_l = v_l.reshape(bsz, t, N_HEADS, V_DIM)
        k_c = k_c.reshape(bsz, n_ctx, N_HEADS, 2, HEAD_DIM)
        v_c = v_c.reshape(bsz, n_ctx, N_HEADS, V_DIM)

        keys = jnp.concatenate([k_l, k_c], axis=1)
        vals = jnp.concatenate([v_l, v_c], axis=1)
        y_attn_l = diff_head_out(diff_attention_blocks(q_l, keys, vals, lam), g_subln[l], lam_init, w_attn_out[l])
        y_conv_l = conv_branch(a_l, b_l, w_dw[l], b_dw[l], ln_g_conv[l], ln_b_conv[l], w_conv_out[l], b_conv_out[l])
        mix_l = (jax.nn.sigmoid(ga_l) * y_attn_l + jax.nn.sigmoid(gc_l) * y_conv_l) @ w_out[l]

        if not last:
            q_c = q_c.reshape(bsz, n_ctx, N_HEADS, 2, HEAD_DIM)
            w_cc = diff_weights(q_c, k_c, lam)
            o_c = jnp.einsum("bhqk,bkhe->bqhe", w_cc.astype(v_c.dtype), v_c)
            y_attn_c = diff_head_out(o_c, g_subln[l], lam_init, w_attn_out[l])
            y_conv_c = conv_branch(a_c, b_c, w_dw[l], b_dw[l], ln_g_conv[l], ln_b_conv[l], w_conv_out[l], b_conv_out[l])
            mix_c = (jax.nn.sigmoid(ga_c) * y_attn_c + jax.nn.sigmoid(gc_c) * y_conv_c) @ w_out[l]
            ctx = ctx + cg_m * mix_c
            hf_c = modulate(rmsnorm(ctx, g_norm_ffn[l]), csh_f, csc_f)
            ctx = ctx + cg_f * ec_moe(hf_c, w_router[l], w_expert_gate[l], w_expert_up[l], w_expert_down[l])

        x = x + g_m * mix_l

        hf_l = modulate(rmsnorm(x, g_norm_ffn[l]), sh_f, sc_f)
        x = x + g_f * ec_moe(hf_l, w_router[l], w_expert_gate[l], w_expert_up[l], w_expert_down[l])

    return rmsnorm(x, g_final)
```

```python
import functools
import math

import jax
import jax.numpy as jnp
from jax import lax
from jax.experimental import pallas as pl
from jax.experimental.pallas import tpu as pltpu

N_HEADS = 8
HEAD_DIM = 64
V_DIM = 2 * HEAD_DIM
CONV_WIDTH = 31
CONV_HALF = CONV_WIDTH // 2
N_MOD = 6
GRID_W = 64
ROPE_THETA = 10000.0
CAPACITY_FACTOR = 2
EPS = 1e-6

LANES = 128
HALO = 16
VMEM_LIMIT = 56 * 1024 * 1024

F32 = jnp.float32
BF16 = jnp.bfloat16


def _cparams(sem):
    return pltpu.CompilerParams(dimension_semantics=sem, vmem_limit_bytes=VMEM_LIMIT)


def _tile(n, pref):
    t = min(n, pref)
    while n % t:
        t //= 2
    return t


def _split_bf16(a):
    hi = a.astype(BF16)
    lo = (a - hi.astype(F32)).astype(BF16)
    return hi, lo


def _dot3(a, b, dims):
    ah, al = _split_bf16(a)
    bh, bl = _split_bf16(b)
    dg = functools.partial(lax.dot_general, dimension_numbers=dims, preferred_element_type=F32)
    return dg(ah, bh) + (dg(ah, bl) + dg(al, bh))


_NN = (((1,), (0,)), ((), ()))
_NT = (((1,), (1,)), ((), ()))


def _sigmoid(x):
    return 1.0 / (1.0 + jnp.exp(-x))


def _adaln_kernel(c_ref, w_ref, b_ref, o_ref):
    c = c_ref[...]
    s = c * _sigmoid(c)
    o_ref[...] = _dot3(s, w_ref[...], _NN) + b_ref[...]


def _adaln(cc, w, b):
    rows, d = cc.shape
    n = w.shape[1]
    tn = _tile(n, 1024)
    return pl.pallas_call(
        _adaln_kernel,
        grid=(n // tn,),
        in_specs=[pl.BlockSpec((rows, d), lambda j: (0, 0)),
                  pl.BlockSpec((d, tn), lambda j: (0, j)),
                  pl.BlockSpec((1, tn), lambda j: (0, j))],
        out_specs=pl.BlockSpec((rows, tn), lambda j: (0, j)),
        out_shape=jax.ShapeDtypeStruct((rows, n), F32),
        compiler_params=_cparams(("arbitrary",)),
        name="adaln",
    )(cc, w, b)


def _norm_mod(x_ref, mod_ref, g_ref, h_scr):
    x = x_ref[...]
    ms = jnp.mean(x * x, axis=-1, keepdims=True)
    y = x * lax.rsqrt(ms + EPS) * g_ref[...]
    h = y * (1.0 + mod_ref[0, 1:2, :]) + mod_ref[0, 0:1, :]
    h_scr[...] = h.astype(BF16)


def _proj_glu_kernel(x_ref, mod_ref, g_ref, wa_ref, wb_ref, o_ref, h_scr):
    @pl.when(pl.program_id(1) == 0)
    def _():
        _norm_mod(x_ref, mod_ref, g_ref, h_scr)

    h = h_scr[...]
    a = jnp.dot(h, wa_ref[...], preferred_element_type=F32)
    b = jnp.dot(h, wb_ref[...], preferred_element_type=F32)
    o_ref[...] = (a * _sigmoid(b)).astype(o_ref.dtype)


def _proj_gate_kernel(x_ref, mod_ref, g_ref, w_ref, o_ref, h_scr):
    @pl.when(pl.program_id(1) == 0)
    def _():
        _norm_mod(x_ref, mod_ref, g_ref, h_scr)

    acc = jnp.dot(h_scr[...], w_ref[...], preferred_element_type=F32)
    o_ref[0] = _sigmoid(acc).astype(o_ref.dtype)


def _rope(acc, cos, sin_signed):
    lane = lax.broadcasted_iota(jnp.int32, (acc.shape[0], LANES), 1)
    first_half = (lane % 32) < 16
    outs = []
    for hh in range(acc.shape[1] // LANES):
        seg = acc[:, hh * LANES:(hh + 1) * LANES]
        partner = jnp.where(first_half, pltpu.roll(seg, LANES - 16, 1), pltpu.roll(seg, 16, 1))
        outs.append(seg * cos + partner * sin_signed)
    return jnp.concatenate(outs, axis=1)


def _proj_qkv_kernel(x_ref, mod_ref, g_ref, w_ref, cos_ref, sin_ref, o_ref, h_scr, *, n_rope, scale_first):
    j = pl.program_id(1)

    @pl.when(j == 0)
    def _():
        _norm_mod(x_ref, mod_ref, g_ref, h_scr)

    acc = jnp.dot(h_scr[...], w_ref[...], preferred_element_type=F32)
    if scale_first:
        acc = acc * jnp.where(j == 0, HEAD_DIM ** -0.5, 1.0)
    if n_rope:
        @pl.when(j < n_rope)
        def _():
            o_ref[0] = _rope(acc, cos_ref[...], sin_ref[...]).astype(o_ref.dtype)

        @pl.when(j >= n_rope)
        def _():
            o_ref[0] = acc.astype(o_ref.dtype)
    else:
        o_ref[0] = acc.astype(o_ref.dtype)


def _proj_common_specs(tm, d, rows_per_mod):
    return [pl.BlockSpec((tm, d), lambda i, j: (i, 0)),
            pl.BlockSpec((1, N_MOD, d), lambda i, j: ((i * tm) // rows_per_mod, 0, 0)),
            pl.BlockSpec((1, d), lambda i, j: (0, 0))]


def _proj_glu(x2, mod3, g, w_bf, rows_per_mod, d_conv):
    m, d = x2.shape
    tm = _tile(min(m, rows_per_mod), 512)
    tn = _tile(d_conv, 512)
    nb = d_conv // tn
    return pl.pallas_call(
        _proj_glu_kernel,
        grid=(m // tm, nb),
        in_specs=_proj_common_specs(tm, d, rows_per_mod) + [
            pl.BlockSpec((d, tn), lambda i, j: (0, j)),
            pl.BlockSpec((d, tn), lambda i, j: (0, nb + j))],
        out_specs=pl.BlockSpec((tm, tn), lambda i, j: (i, j)),
        out_shape=jax.ShapeDtypeStruct((m, d_conv), BF16),
        scratch_shapes=[pltpu.VMEM((tm, d), BF16)],
        compiler_params=_cparams(("parallel", "arbitrary")),
        name="proj_glu",
    )(x2, mod3, g, w_bf, w_bf)


def _proj_gate(x2, mod3, g, w_bf, rows_per_mod, col0, width):
    m, d = x2.shape
    tm = _tile(min(m, rows_per_mod), 512)
    b0 = col0 // width
    return pl.pallas_call(
        _proj_gate_kernel,
        grid=(m // tm, 2),
        in_specs=_proj_common_specs(tm, d, rows_per_mod) + [
            pl.BlockSpec((d, width), lambda i, j: (0, b0 + j))],
        out_specs=pl.BlockSpec((1, tm, width), lambda i, j: (j, i, 0)),
        out_shape=jax.ShapeDtypeStruct((2, m, width), BF16),
        scratch_shapes=[pltpu.VMEM((tm, d), BF16)],
        compiler_params=_cparams(("parallel", "arbitrary")),
        name="proj_gate",
    )(x2, mod3, g, w_bf)


def _proj_qkv(x2, mod3, g, w_bf, cos, sin_signed, rows_per_mod, col0, width, ngroups, n_rope, scale_first):
    m, d = x2.shape
    t = cos.shape[0]
    tm = _tile(min(m, rows_per_mod, t), 512)
    b0 = col0 // width
    tblocks = t // tm
    kern = functools.partial(_proj_qkv_kernel, n_rope=n_rope, scale_first=scale_first)
    return pl.pallas_call(
        kern,
        grid=(m // tm, ngroups),
        in_specs=_proj_common_specs(tm, d, rows_per_mod) + [
            pl.BlockSpec((d, width), lambda i, j: (0, b0 + j)),
            pl.BlockSpec((tm, LANES), lambda i, j: (i % tblocks, 0)),
            pl.BlockSpec((tm, LANES), lambda i, j: (i % tblocks, 0))],
        out_specs=pl.BlockSpec((1, tm, width), lambda i, j: (j, i, 0)),
        out_shape=jax.ShapeDtypeStruct((ngroups, m, width), BF16),
        scratch_shapes=[pltpu.VMEM((tm, d), BF16)],
        compiler_params=_cparams(("parallel", "arbitrary")),
        name="proj_qkv",
    )(x2, mod3, g, w_bf, cos, sin_signed)


def _conv_kernel(up_ref, um_ref, un_ref, gc_ref, wdw_ref, bdw_ref, lng_ref, lnb_ref, wc_ref, bc_ref,
                 o_ref, buf, cv, *, tt, rb):
    ti = pl.program_id(1)
    nt = pl.num_programs(1)
    d = um_ref.shape[2]
    prev = up_ref[0].astype(F32)
    nxt = un_ref[0].astype(F32)
    buf[0:HALO, :] = jnp.where(ti > 0, prev, 0.0)
    buf[HALO:HALO + tt, :] = um_ref[0].astype(F32)
    buf[HALO + tt:2 * HALO + tt, :] = jnp.where(ti < nt - 1, nxt, 0.0)

    off = HALO - CONV_HALF
    for c in range(d // LANES):
        cs = slice(c * LANES, (c + 1) * LANES)
        for r in range(tt // rb):
            acc = jnp.zeros((rb, LANES), F32)
            for k in range(CONV_WIDTH):
                acc = acc + buf[r * rb + k + off:r * rb + k + off + rb, cs] * wdw_ref[k:k + 1, cs]
            cv[r * rb:(r + 1) * rb, cs] = acc

    v = cv[...] + bdw_ref[...]
    mu = jnp.mean(v, axis=-1, keepdims=True)
    xc = v - mu
    var = jnp.mean(xc * xc, axis=-1, keepdims=True)
    y = xc * lax.rsqrt(var + EPS) * lng_ref[...] + lnb_ref[...]
    z = y * _sigmoid(y)
    yc = jnp.dot(z.astype(BF16), wc_ref[...], preferred_element_type=F32) + bc_ref[...]
    o_ref[...] = (gc_ref[0].astype(F32) * yc).astype(o_ref.dtype)


def _conv_branch(u3, gates, wdw, bdw, lng, lnb, wc_bf, bc):
    bsz, t, dc = u3.shape
    d = wc_bf.shape[1]
    tt = _tile(t, 256)
    nt = t // tt
    hb = tt // HALO
    nhb = t // HALO
    kern = functools.partial(_conv_kernel, tt=tt, rb=_tile(tt, 64))
    const = lambda shape: pl.BlockSpec(shape, lambda b, i: (0,) * len(shape))
    return pl.pallas_call(
        kern,
        grid=(bsz, nt),
        in_specs=[pl.BlockSpec((1, HALO, dc), lambda b, i: (b, jnp.maximum(i * hb - 1, 0), 0)),
                  pl.BlockSpec((1, tt, dc), lambda b, i: (b, i, 0)),
                  pl.BlockSpec((1, HALO, dc), lambda b, i: (b, jnp.minimum((i + 1) * hb, nhb - 1), 0)),
                  pl.BlockSpec((1, tt, d), lambda b, i: (0, b * nt + i, 0)),
                  const((CONV_WIDTH, dc)), const((1, dc)), const((1, dc)), const((1, dc)),
                  const((dc, d)), const((1, d))],
        out_specs=pl.BlockSpec((tt, d), lambda b, i: (b * nt + i, 0)),
        out_shape=jax.ShapeDtypeStruct((bsz * t, d), BF16),
        scratch_shapes=[pltpu.VMEM((tt + 2 * HALO, dc), F32), pltpu.VMEM((tt, dc), F32)],
        compiler_params=_cparams(("parallel", "parallel")),
        name="conv",
    )(u3, u3, u3, gates, wdw, bdw, lng, lnb, wc_bf, bc)


def _attn_kernel(lam_ref, q_ref, kl_ref, vl_ref, kc_ref, vc_ref, gs_ref, o_ref, *, tq, tk, out_scale):
    q = q_ref[0]
    lane = lax.broadcasted_iota(jnp.int32, q.shape, 1)
    zero = jnp.zeros_like(q)
    qs = jnp.concatenate([jnp.where(lane < HEAD_DIM, q, zero), jnp.where(lane >= HEAD_DIM, q, zero)], axis=0)

    def step(k, v, carry):
        m, l, acc = carry
        s = lax.dot_general(qs, k, _NT, preferred_element_type=F32)
        m_new = jnp.maximum(m, jnp.max(s, axis=-1, keepdims=True))
        alpha = jnp.exp(m - m_new)
        p = jnp.exp(s - m_new)
        l = alpha * l + jnp.sum(p, axis=-1, keepdims=True)
        acc = alpha * acc + jnp.dot(p.astype(BF16), v, preferred_element_type=F32)
        return m_new, l, acc

    init = (jnp.full((2 * tq, 1), -1e30, F32), jnp.zeros((2 * tq, 1), F32), jnp.zeros((2 * tq, V_DIM), F32))

    def body(i, carry):
        ks = pl.ds(pl.multiple_of(i * tk, tk), tk)
        return step(kl_ref[0, ks, :], vl_ref[0, ks, :], carry)

    carry = lax.fori_loop(0, kl_ref.shape[1] // tk, body, init)
    m, l, acc = step(kc_ref[0], vc_ref[0], carry)
    o = acc / l
    o = o[:tq] - lam_ref[0] * o[tq:]
    ms = jnp.mean(o * o, axis=-1, keepdims=True)
    o_ref[...] = (o * lax.rsqrt(ms + EPS) * gs_ref[...] * out_scale).astype(o_ref.dtype)


def _attention(lam, qkv, kvc, g_subln, bsz, t, n_ctx, out_scale):
    m, width = qkv.shape[1], qkv.shape[2]
    nh = width // V_DIM
    tq = _tile(t, 256)
    tk = _tile(t, 512)
    nq = t // tq
    kern = functools.partial(_attn_kernel, tq=tq, tk=tk, out_scale=out_scale)
    return pl.pallas_call(
        kern,
        grid=(bsz, nh, nq),
        in_specs=[pl.BlockSpec(memory_space=pltpu.SMEM),
                  pl.BlockSpec((1, tq, V_DIM), lambda b, h, i: (0, b * nq + i, h)),
                  pl.BlockSpec((1, t, V_DIM), lambda b, h, i: (1, b, h)),
                  pl.BlockSpec((1, t, V_DIM), lambda b, h, i: (2, b, h)),
                  pl.BlockSpec((1, n_ctx, V_DIM), lambda b, h, i: (0, b, h)),
                  pl.BlockSpec((1, n_ctx, V_DIM), lambda b, h, i: (1, b, h)),
                  pl.BlockSpec((1, V_DIM), lambda b, h, i: (0, 0))],
        out_specs=pl.BlockSpec((tq, V_DIM), lambda b, h, i: (b * nq + i, h)),
        out_shape=jax.ShapeDtypeStruct((m, width), BF16),
        compiler_params=_cparams(("parallel", "parallel", "arbitrary")),
        name="attn",
    )(lam, qkv, qkv, qkv, kvc, kvc, g_subln)


def _post_kernel(o_ref, ga_ref, gy_ref, x_ref, mod_ref, wa_ref, wo_ref, gf_ref, wrt_ref,
                 xmid_ref, hf_ref, aff_ref):
    y_attn = jnp.dot(o_ref[...], wa_ref[...], preferred_element_type=F32)
    merged = ga_ref[0].astype(F32) * y_attn + gy_ref[...].astype(F32)
    mix = jnp.dot(merged.astype(BF16), wo_ref[...], preferred_element_type=F32)
    xm = x_ref[...] + mod_ref[0, 2:3, :] * mix
    xmid_ref[...] = xm
    ms = jnp.mean(xm * xm, axis=-1, keepdims=True)
    hf = xm * lax.rsqrt(ms + EPS) * gf_ref[...]
    hf = hf * (1.0 + mod_ref[0, 4:5, :]) + mod_ref[0, 3:4, :]
    hf_ref[...] = hf.astype(hf_ref.dtype)
    logits = _dot3(wrt_ref[...], hf, _NT)
    z = jnp.exp(logits - jnp.max(logits, axis=0, keepdims=True))
    aff_ref[0] = z / jnp.sum(z, axis=0, keepdims=True)


def _post(o, gates, gy, x2, mod3, wa_bf, wo_bf, gf, wrt, bsz, t):
    m, d = x2.shape
    aw = o.shape[1]
    ne = wrt.shape[0]
    tm = _tile(t, 512)
    nt = t // tm
    const = lambda shape: pl.BlockSpec(shape, lambda i: (0,) * len(shape))
    return pl.pallas_call(
        _post_kernel,
        grid=(m // tm,),
        in_specs=[pl.BlockSpec((tm, aw), lambda i: (i, 0)),
                  pl.BlockSpec((1, tm, d), lambda i: (1, i, 0)),
                  pl.BlockSpec((tm, d), lambda i: (i, 0)),
                  pl.BlockSpec((tm, d), lambda i: (i, 0)),
                  pl.BlockSpec((1, N_MOD, d), lambda i: (i // nt, 0, 0)),
                  const((aw, d)), const((d, d)), const((1, d)), const((ne, d))],
        out_specs=[pl.BlockSpec((tm, d), lambda i: (i, 0)),
                   pl.BlockSpec((tm, d), lambda i: (i, 0)),
                   pl.BlockSpec((1, ne, tm), lambda i: (i // nt, 0, i % nt))],
        out_shape=[jax.ShapeDtypeStruct((m, d), F32),
                   jax.ShapeDtypeStruct((m, d), BF16),
                   jax.ShapeDtypeStruct((bsz, ne, t), F32)],
        compiler_params=_cparams(("parallel",)),
        name="post",
    )(o, gates, gy, x2, mod3, wa_bf, wo_bf, gf, wrt)


def _count(mask):
    ones = jnp.where(mask, 1.0, 0.0)
    return jnp.sum(jnp.sum(ones, axis=2, keepdims=True), axis=1, keepdims=True)


def _route_kernel(a_ref, pos_ref, *, cap):
    a = a_ref[0]
    ne, nc, ln = a.shape
    rows = ne * nc
    bits = lax.bitcast_convert_type(a, jnp.int32)

    def search(i, cur):
        cand = cur | jnp.left_shift(jnp.int32(1), 30 - i)
        return jnp.where(_count(bits >= cand) >= cap, cand, cur)

    tau = lax.fori_loop(0, 31, search, jnp.zeros((ne, 1, 1), jnp.int32))

    r_i = lax.broadcasted_iota(jnp.int32, (rows, rows), 0)
    r_j = lax.broadcasted_iota(jnp.int32, (rows, rows), 1)
    shift = int(math.log2(nc))
    same_expert = lax.shift_right_logical(r_i, shift) == lax.shift_right_logical(r_j, shift)
    chunk_before = jnp.where(same_expert & (r_j < r_i), 1.0, 0.0).astype(BF16)
    l_i = lax.broadcasted_iota(jnp.int32, (ln, ln), 0)
    l_j = lax.broadcasted_iota(jnp.int32, (ln, ln), 1)
    lane_upto = jnp.where(l_i <= l_j, 1.0, 0.0).astype(BF16)

    def excl_prefix(mask):
        xm = jnp.where(mask, 1.0, 0.0).reshape(rows, ln)
        incl = jnp.dot(xm.astype(BF16), lane_upto, preferred_element_type=F32)
        tot = jnp.broadcast_to(incl[:, ln - 1:ln], (rows, ln))
        offs = jnp.dot(chunk_before, tot.astype(BF16), preferred_element_type=F32)
        return (incl + offs - xm).reshape(ne, nc, ln)

    gt = bits > tau
    eq = bits == tau
    need = cap - _count(gt)
    sel = gt | (eq & (excl_prefix(eq) < need))
    pos = excl_prefix(sel)
    pos_ref[0] = jnp.where(sel, pos, -1.0).astype(jnp.int32)


def _route(aff4, cap):
    bsz, ne, nc, ln = aff4.shape
    assert nc & (nc - 1) == 0, "token chunks per sample must be a power of two"
    return pl.pallas_call(
        functools.partial(_route_kernel, cap=cap),
        grid=(bsz,),
        in_specs=[pl.BlockSpec((1, ne, nc, ln), lambda b: (b, 0, 0, 0))],
        out_specs=pl.BlockSpec((1, ne, nc, ln), lambda b: (b, 0, 0, 0)),
        out_shape=jax.ShapeDtypeStruct((bsz, ne, nc, ln), jnp.int32),
        compiler_params=_cparams(("parallel",)),
        name="route",
    )(aff4)


def _gather_kernel(pos_ref, aff_ref, hf_ref, xs_ref, gate_ref, *, cap, tc):
    t = hf_ref.shape[1]
    d = hf_ref.shape[2]
    slot = lax.broadcasted_iota(jnp.int32, (cap, tc), 0)
    xs = jnp.zeros((cap, d), F32)
    gate = jnp.zeros((cap, 1), F32)
    for c in range(t // tc):
        cs = slice(c * tc, (c + 1) * tc)
        hit = pos_ref[0, 0, :, cs] == slot
        xs = xs + jnp.dot(jnp.where(hit, 1.0, 0.0).astype(BF16), hf_ref[0, cs, :], preferred_element_type=F32)
        gate = gate + jnp.sum(jnp.where(hit, aff_ref[0, 0, :, cs], 0.0), axis=1, keepdims=True)
    xs_ref[0, 0] = xs.astype(xs_ref.dtype)
    gate_ref[0, 0] = gate


def _gather(pos4, aff4, hf3, cap):
    bsz, ne, _, t = pos4.shape
    d = hf3.shape[2]
    kern = functools.partial(_gather_kernel, cap=cap, tc=_tile(t, 512))
    return pl.pallas_call(
        kern,
        grid=(bsz, ne),
        in_specs=[pl.BlockSpec((1, 1, 1, t), lambda b, e: (b, e, 0, 0)),
                  pl.BlockSpec((1, 1, 1, t), lambda b, e: (b, e, 0, 0)),
                  pl.BlockSpec((1, t, d), lambda b, e: (b, 0, 0))],
        out_specs=[pl.BlockSpec((1, 1, cap, d), lambda b, e: (e, b, 0, 0)),
                   pl.BlockSpec((1, 1, cap, 1), lambda b, e: (e, b, 0, 0))],
        out_shape=[jax.ShapeDtypeStruct((ne, bsz, cap, d), BF16),
                   jax.ShapeDtypeStruct((ne, bsz, cap, 1), F32)],
        compiler_params=_cparams(("parallel", "arbitrary")),
        name="gather",
    )(pos4, aff4, hf3)


def _expert_kernel(xs_ref, gate_ref, wg_ref, wu_ref, wd_ref, o_ref, acc):
    f = pl.program_id(2)

    @pl.when(f == 0)
    def _():
        acc[...] = jnp.zeros_like(acc)

    x = xs_ref[0]
    g = jnp.dot(x, wg_ref[0].astype(BF16), preferred_element_type=F32)
    u = jnp.dot(x, wu_ref[0].astype(BF16), preferred_element_type=F32)
    hid = (g * _sigmoid(g)) * u
    acc[...] += jnp.dot(hid.astype(BF16), wd_ref[0].astype(BF16), preferred_element_type=F32)

    @pl.when(f == pl.num_programs(2) - 1)
    def _():
        o_ref[0] = (acc[...] * gate_ref[0]).astype(o_ref.dtype)


def _experts(xs3, gate3, wg, wu, wd):
    ne, r, d = xs3.shape
    ff = wg.shape[2]
    tr = _tile(r, 2048)
    tf = _tile(ff, 256)
    return pl.pallas_call(
        _expert_kernel,
        grid=(ne, r // tr, ff // tf),
        in_specs=[pl.BlockSpec((1, tr, d), lambda e, i, f: (e, i, 0)),
                  pl.BlockSpec((1, tr, 1), lambda e, i, f: (e, i, 0)),
                  pl.BlockSpec((1, d, tf), lambda e, i, f: (e, 0, f)),
                  pl.BlockSpec((1, d, tf), lambda e, i, f: (e, 0, f)),
                  pl.BlockSpec((1, tf, d), lambda e, i, f: (e, f, 0))],
        out_specs=pl.BlockSpec((1, tr, d), lambda e, i, f: (e, i, 0)),
        out_shape=jax.ShapeDtypeStruct((ne, r, d), BF16),
        scratch_shapes=[pltpu.VMEM((tr, d), F32)],
        compiler_params=_cparams(("parallel", "parallel", "arbitrary")),
        name="experts",
    )(xs3, gate3, wg, wu, wd)


def _combine_kernel(pos_ref, ys_ref, xm_ref, mod_ref, gfin_ref, o_ref, acc, pos_t, *, cap):
    e = pl.program_id(2)
    ne, tt = pos_ref.shape[1], pos_ref.shape[2]

    @pl.when(e == 0)
    def _():
        acc[...] = jnp.zeros_like(acc)
        p = pos_ref[0].astype(F32)
        padded = jnp.concatenate([p, jnp.full((LANES - ne, tt), -1.0, F32)], axis=0)
        pos_t[...] = padded.T

    lane = lax.broadcasted_iota(jnp.int32, (tt, LANES), 1)
    col = jnp.sum(jnp.where(lane == e, pos_t[...], 0.0), axis=1, keepdims=True)
    slot = lax.broadcasted_iota(jnp.int32, (tt, cap), 1).astype(F32)
    onehot = jnp.where(col == slot, 1.0, 0.0).astype(BF16)
    acc[...] += jnp.dot(onehot, ys_ref[0, 0], preferred_element_type=F32)

    @pl.when(e == ne - 1)
    def _():
        xo = xm_ref[...] + mod_ref[0, 5:6, :] * acc[...]
        ms = jnp.mean(xo * xo, axis=-1, keepdims=True)
        o_ref[...] = xo * lax.rsqrt(ms + EPS) * gfin_ref[...]


def _combine(pos3, ys4, xmid, mod3, gfin, cap):
    bsz, ne, t = pos3.shape
    m, d = xmid.shape
    tt = _tile(t, 1024)
    nt = t // tt
    return pl.pallas_call(
        functools.partial(_combine_kernel, cap=cap),
        grid=(bsz, nt, ne),
        in_specs=[pl.BlockSpec((1, ne, tt), lambda b, i, e: (b, 0, i)),
                  pl.BlockSpec((1, 1, cap, d), lambda b, i, e: (e, b, 0, 0)),
                  pl.BlockSpec((tt, d), lambda b, i, e: (b * nt + i, 0)),
                  pl.BlockSpec((1, N_MOD, d), lambda b, i, e: (b, 0, 0)),
                  pl.BlockSpec((1, d), lambda b, i, e: (0, 0))],
        out_specs=pl.BlockSpec((tt, d), lambda b, i, e: (b * nt + i, 0)),
        out_shape=jax.ShapeDtypeStruct((m, d), F32),
        scratch_shapes=[pltpu.VMEM((tt, d), F32), pltpu.VMEM((tt, LANES), F32)],
        compiler_params=_cparams(("parallel", "parallel", "arbitrary")),
        name="combine",
    )(pos3, ys4, xmid, mod3, gfin)


def _rope_tables(t):
    n_freq = HEAD_DIM // 4
    tok = jnp.arange(t)
    inv = ROPE_THETA ** (-jnp.arange(n_freq, dtype=F32) / n_freq)
    ang_r = (tok // GRID_W)[:, None].astype(F32) * inv
    ang_c = (tok % GRID_W)[:, None].astype(F32) * inv
    cos64 = jnp.concatenate([jnp.cos(ang_r)] * 2 + [jnp.cos(ang_c)] * 2, axis=-1)
    sin64 = jnp.concatenate([-jnp.sin(ang_r), jnp.sin(ang_r), -jnp.sin(ang_c), jnp.sin(ang_c)], axis=-1)
    return jnp.tile(cos64, (1, 2)), jnp.tile(sin64, (1, 2))


def kernel(x, c, ctx, c_ctx, w_ada, b_ada, g_norm_mix, g_norm_ffn, w_in, w_dw, b_dw, ln_g_conv, ln_b_conv,
           w_conv_out, b_conv_out, lambda_q1, lambda_k1, lambda_q2, lambda_k2, g_subln, w_attn_out, w_out,
           w_router, w_expert_gate, w_expert_up, w_expert_down, g_final):
    bsz, t, d = x.shape
    n_ctx = ctx.shape[1]
    depth = w_ada.shape[0]
    assert depth == 1, "single-layer trunk: the context stream only feeds keys/values"
    ne = w_router.shape[2]
    cap = CAPACITY_FACTOR * t // ne
    dc = w_dw.shape[2]
    qkw = N_HEADS * 2 * HEAD_DIM
    aw = N_HEADS * V_DIM
    assert dc == d and qkw == d and aw == d, "projection groups are addressed as equal-width column blocks"
    layer = 0
    lam_init = 0.8 - 0.6 * math.exp(-0.3 * layer)
    lam = (jnp.exp(jnp.sum(lambda_q1[layer].astype(F32) * lambda_k1[layer].astype(F32)))
           - jnp.exp(jnp.sum(lambda_q2[layer].astype(F32) * lambda_k2[layer].astype(F32)))
           + lam_init).reshape(1)

    row = lambda v: v.reshape(1, -1)
    x2 = x.reshape(bsz * t, d)
    ctx2 = ctx.reshape(bsz * n_ctx, d)

    pad = (-(bsz + 1)) % 8
    cc = jnp.concatenate([c, c_ctx[None, :], jnp.zeros((pad, d), F32)], axis=0)
    mod = _adaln(cc, w_ada[layer], row(b_ada[layer]))
    mod_lat = mod[:bsz].reshape(bsz, N_MOD, d)
    mod_ctx = mod[bsz:bsz + 1].reshape(1, N_MOD, d)

    w_in_bf = w_in[layer].astype(BF16)
    g_mix = row(g_norm_mix[layer])
    cos, sin_signed = _rope_tables(t)

    u = _proj_glu(x2, mod_lat, g_mix, w_in_bf, t, dc)
    qkv = _proj_qkv(x2, mod_lat, g_mix, w_in_bf, cos, sin_signed, t, 2 * dc, d, 3, 2, True)
    kvc = _proj_qkv(ctx2, mod_ctx, g_mix, w_in_bf, cos, sin_signed, bsz * n_ctx, 2 * dc + qkw, d, 2, 0, False)
    gates = _proj_gate(x2, mod_lat, g_mix, w_in_bf, t, 2 * dc + 2 * qkw + aw, d)

    gy = _conv_branch(u.reshape(bsz, t, dc), gates, w_dw[layer], row(b_dw[layer]), row(ln_g_conv[layer]),
                      row(ln_b_conv[layer]), w_conv_out[layer].astype(BF16), row(b_conv_out[layer]))
    o = _attention(lam, qkv, kvc, row(g_subln[layer]), bsz, t, n_ctx, 1.0 - lam_init)

    xmid, hf, aff_t = _post(o, gates, gy, x2, mod_lat, w_attn_out[layer].astype(BF16), w_out[layer].astype(BF16),
                            row(g_norm_ffn[layer]), w_router[layer].T, bsz, t)

    pos = _route(aff_t.reshape(bsz, ne, t // LANES, LANES), cap)
    xs, gate = _gather(pos.reshape(bsz, ne, 1, t), aff_t.reshape(bsz, ne, 1, t), hf.reshape(bsz, t, d), cap)
    ys = _experts(xs.reshape(ne, bsz * cap, d), gate.reshape(ne, bsz * cap, 1),
                  w_expert_gate[layer], w_expert_up[layer], w_expert_down[layer])
    out = _combine(pos.reshape(bsz, ne, t), ys.reshape(ne, bsz, cap, d), xmid, mod_lat, row(g_final), cap)
    return out.reshape(bsz, t, d)
```

```python
import functools
import math

import jax
import jax.numpy as jnp
from jax import lax
from jax.experimental import pallas as pl
from jax.experimental.pallas import tpu as pltpu

N_HEADS = 8
HEAD_DIM = 64
V_DIM = 2 * HEAD_DIM
CONV_WIDTH = 31
CONV_HALF = CONV_WIDTH // 2
N_MOD = 6
GRID_W = 64
ROPE_THETA = 10000.0
CAPACITY_FACTOR = 2
EPS = 1e-6

LANES = 128
SUBLANES = 8
HALO = 16
VMEM_LIMIT = 56 * 1024 * 1024

F32 = jnp.float32
BF16 = jnp.bfloat16


def _cparams(sem):
    return pltpu.CompilerParams(dimension_semantics=sem, vmem_limit_bytes=VMEM_LIMIT)


def _tile(n, pref):
    t = min(n, pref)
    while n % t:
        t //= 2
    return t


def _split_bf16(a):
    hi = a.astype(BF16)
    lo = (a - hi.astype(F32)).astype(BF16)
    return hi, lo


def _dot3(a, b, dims):
    ah, al = _split_bf16(a)
    bh, bl = _split_bf16(b)
    dg = functools.partial(lax.dot_general, dimension_numbers=dims, preferred_element_type=F32)
    return dg(ah, bh) + (dg(ah, bl) + dg(al, bh))


_NN = (((1,), (0,)), ((), ()))
_NT = (((1,), (1,)), ((), ()))


def _sigmoid(x):
    return 1.0 / (1.0 + jnp.exp(-x))


def _adaln_kernel(c_ref, w_ref, b_ref, o_ref):
    c = c_ref[...]
    s = c * _sigmoid(c)
    o_ref[...] = _dot3(s, w_ref[...], _NN) + b_ref[...]


def _adaln(cc, w, b):
    rows, d = cc.shape
    n = w.shape[1]
    tn = _tile(n, 1024)
    return pl.pallas_call(
        _adaln_kernel,
        grid=(n // tn,),
        in_specs=[pl.BlockSpec((rows, d), lambda j: (0, 0)),
                  pl.BlockSpec((d, tn), lambda j: (0, j)),
                  pl.BlockSpec((1, tn), lambda j: (0, j))],
        out_specs=pl.BlockSpec((rows, tn), lambda j: (0, j)),
        out_shape=jax.ShapeDtypeStruct((rows, n), F32),
        compiler_params=_cparams(("arbitrary",)),
        name="adaln",
    )(cc, w, b)


def _norm_mod(x_ref, mod_ref, g_ref):
    x = x_ref[...]
    ms = jnp.mean(x * x, axis=-1, keepdims=True)
    y = x * lax.rsqrt(ms + EPS) * g_ref[...]
    return (y * (1.0 + mod_ref[0, 1:2, :]) + mod_ref[0, 0:1, :]).astype(BF16)


def _rope(acc, cos, sin_signed):
    lane = lax.broadcasted_iota(jnp.int32, (acc.shape[0], LANES), 1)
    first_half = (lane % 32) < 16
    outs = []
    for hh in range(acc.shape[1] // LANES):
        seg = acc[:, hh * LANES:(hh + 1) * LANES]
        partner = jnp.where(first_half, pltpu.roll(seg, LANES - 16, 1), pltpu.roll(seg, 16, 1))
        outs.append(seg * cos + partner * sin_signed)
    return jnp.concatenate(outs, axis=1)


def _proj_ctx_kernel(x_ref, mod_ref, g_ref, wk_ref, wv_ref, o_ref):
    h = _norm_mod(x_ref, mod_ref, g_ref)
    o_ref[0] = jnp.dot(h, wk_ref[...], preferred_element_type=F32).astype(o_ref.dtype)
    o_ref[1] = jnp.dot(h, wv_ref[...], preferred_element_type=F32).astype(o_ref.dtype)


def _proj_ctx(x2, mod3, g, w_bf, k_group):
    m, d = x2.shape
    tm = _tile(m, 512)
    return pl.pallas_call(
        _proj_ctx_kernel,
        grid=(m // tm,),
        in_specs=[pl.BlockSpec((tm, d), lambda i: (i, 0)),
                  pl.BlockSpec((1, N_MOD, d), lambda i: (0, 0, 0)),
                  pl.BlockSpec((1, d), lambda i: (0, 0)),
                  pl.BlockSpec((d, d), lambda i: (0, k_group)),
                  pl.BlockSpec((d, d), lambda i: (0, k_group + 1))],
        out_specs=pl.BlockSpec((2, tm, d), lambda i: (0, i, 0)),
        out_shape=jax.ShapeDtypeStruct((2, m, d), BF16),
        compiler_params=_cparams(("parallel",)),
        name="proj_ctx",
    )(x2, mod3, g, w_bf, w_bf)


def _proj_all_kernel(x_ref, mod_ref, g_ref, w_ref, cos_ref, sin_ref, u_ref, qkv_ref, gate_ref):
    d = x_ref.shape[1]
    h = _norm_mod(x_ref, mod_ref, g_ref)

    def group(n):
        return jnp.dot(h, w_ref[:, n * d:(n + 1) * d], preferred_element_type=F32)

    cos = cos_ref[...]
    sin = sin_ref[...]
    u_ref[...] = (group(0) * _sigmoid(group(1))).astype(u_ref.dtype)
    qkv_ref[0] = _rope(group(2) * (HEAD_DIM ** -0.5 * math.log2(math.e)), cos, sin).astype(qkv_ref.dtype)
    qkv_ref[1] = _rope(group(3), cos, sin).astype(qkv_ref.dtype)
    qkv_ref[2] = group(4).astype(qkv_ref.dtype)
    gate_ref[0] = _sigmoid(group(5)).astype(gate_ref.dtype)
    gate_ref[1] = _sigmoid(group(6)).astype(gate_ref.dtype)


def _proj_all(x2, mod3, g, w_bf, cos, sin_signed, t):
    m, d = x2.shape
    tm = _tile(t, 512)
    tblocks = t // tm
    return pl.pallas_call(
        _proj_all_kernel,
        grid=(m // tm,),
        in_specs=[pl.BlockSpec((tm, d), lambda i: (i, 0)),
                  pl.BlockSpec((1, N_MOD, d), lambda i: (i // tblocks, 0, 0)),
                  pl.BlockSpec((1, d), lambda i: (0, 0)),
                  pl.BlockSpec(w_bf.shape, lambda i: (0, 0), pipeline_mode=pl.Buffered(1)),
                  pl.BlockSpec((tm, LANES), lambda i: (i % tblocks, 0)),
                  pl.BlockSpec((tm, LANES), lambda i: (i % tblocks, 0))],
        out_specs=[pl.BlockSpec((tm, d), lambda i: (i, 0)),
                   pl.BlockSpec((3, tm, d), lambda i: (0, i, 0)),
                   pl.BlockSpec((2, tm, d), lambda i: (0, i, 0))],
        out_shape=[jax.ShapeDtypeStruct((m, d), BF16),
                   jax.ShapeDtypeStruct((3, m, d), BF16),
                   jax.ShapeDtypeStruct((2, m, d), BF16)],
        compiler_params=_cparams(("parallel",)),
        name="proj_all",
    )(x2, mod3, g, w_bf, cos, sin_signed)


def _conv_kernel(up_ref, um_ref, un_ref, gc_ref, wdw_ref, bdw_ref, lng_ref, lnb_ref, wc_ref, bc_ref,
                 o_ref, buf, cv, *, tt, rb):
    ti = pl.program_id(1)
    nt = pl.num_programs(1)
    d = um_ref.shape[2]
    prev = up_ref[0].astype(F32)
    nxt = un_ref[0].astype(F32)
    buf[0:HALO, :] = jnp.where(ti > 0, prev, 0.0)
    buf[HALO:HALO + tt, :] = um_ref[0].astype(F32)
    buf[HALO + tt:2 * HALO + tt, :] = jnp.where(ti < nt - 1, nxt, 0.0)

    off = HALO - CONV_HALF
    taps = [[(s - off, s // SUBLANES) for s in range(off, off + CONV_WIDTH) if s % SUBLANES == r]
            for r in range(SUBLANES)]
    n_slab = (off + CONV_WIDTH - 1) // SUBLANES + 1
    for c in range(d // LANES):
        cs = slice(c * LANES, (c + 1) * LANES)
        for rblk in range(tt // rb):
            r0 = rblk * rb
            slabs = [buf[r0 + SUBLANES * a:r0 + SUBLANES * a + rb + SUBLANES, cs] for a in range(n_slab)]
            acc = None
            for r in range(SUBLANES):
                z = None
                for k, a in taps[r]:
                    term = slabs[a] * wdw_ref[k:k + 1, cs]
                    z = term if z is None else z + term
                z = z[r:r + rb]
                acc = z if acc is None else acc + z
            cv[r0:r0 + rb, cs] = acc

    v = cv[...] + bdw_ref[...]
    mu = jnp.mean(v, axis=-1, keepdims=True)
    xc = v - mu
    var = jnp.mean(xc * xc, axis=-1, keepdims=True)
    y = xc * lax.rsqrt(var + EPS) * lng_ref[...] + lnb_ref[...]
    z = y * _sigmoid(y)
    yc = jnp.dot(z.astype(BF16), wc_ref[...], preferred_element_type=F32) + bc_ref[...]
    o_ref[...] = (gc_ref[0].astype(F32) * yc).astype(o_ref.dtype)


def _conv_branch(u3, gates, wdw, bdw, lng, lnb, wc_bf, bc):
    bsz, t, dc = u3.shape
    d = wc_bf.shape[1]
    tt = _tile(t, 256)
    nt = t // tt
    hb = tt // HALO
    nhb = t // HALO
    kern = functools.partial(_conv_kernel, tt=tt, rb=_tile(tt, 64))
    const = lambda shape: pl.BlockSpec(shape, lambda b, i: (0,) * len(shape))
    return pl.pallas_call(
        kern,
        grid=(bsz, nt),
        in_specs=[pl.BlockSpec((1, HALO, dc), lambda b, i: (b, jnp.maximum(i * hb - 1, 0), 0)),
                  pl.BlockSpec((1, tt, dc), lambda b, i: (b, i, 0)),
                  pl.BlockSpec((1, HALO, dc), lambda b, i: (b, jnp.minimum((i + 1) * hb, nhb - 1), 0)),
                  pl.BlockSpec((1, tt, d), lambda b, i: (0, b * nt + i, 0)),
                  const((CONV_WIDTH, dc)), const((1, dc)), const((1, dc)), const((1, dc)),
                  const((dc, d)), const((1, d))],
        out_specs=pl.BlockSpec((tt, d), lambda b, i: (b * nt + i, 0)),
        out_shape=jax.ShapeDtypeStruct((bsz * t, d), BF16),
        scratch_shapes=[pltpu.VMEM((tt + 2 * HALO, dc), F32), pltpu.VMEM((tt, dc), F32)],
        compiler_params=_cparams(("parallel", "parallel")),
        name="conv",
    )(u3, u3, u3, gates, wdw, bdw, lng, lnb, wc_bf, bc)


def _attn_kernel(lam_ref, q_ref, kl_ref, vl_ref, kc_ref, vc_ref, gs_ref, o_ref, vext, s_scr, m_scr,
                 *, tq, out_scale):
    j = pl.program_id(2)
    t = kl_ref.shape[1]

    @pl.when(j == 0)
    def _():
        vext[0:t, 0:V_DIM] = vl_ref[0]
        vext[t:, 0:V_DIM] = vc_ref[0]
        vext[:, V_DIM:] = jnp.ones((vext.shape[0], V_DIM), vext.dtype)
        s_scr[...] = jnp.zeros_like(s_scr)
        m_scr[...] = jnp.zeros_like(m_scr)

    p = jnp.exp2(s_scr[...] - m_scr[...]).astype(BF16)
    acc = jnp.dot(p, vext[...], preferred_element_type=F32)
    o = acc[:, :V_DIM] / acc[:, V_DIM:]
    o = o[:tq] - lam_ref[0] * o[tq:]
    ms = jnp.mean(o * o, axis=-1, keepdims=True)
    o_ref[...] = (o * lax.rsqrt(ms + EPS) * gs_ref[...] * out_scale).astype(o_ref.dtype)

    q = q_ref[0]
    lane = lax.broadcasted_iota(jnp.int32, q.shape, 1)
    zero = jnp.zeros_like(q)
    qs = jnp.concatenate([jnp.where(lane < HEAD_DIM, q, zero), jnp.where(lane >= HEAD_DIM, q, zero)], axis=0)
    s_lat = lax.dot_general(qs, kl_ref[0], _NT, preferred_element_type=F32)
    s_ctx = lax.dot_general(qs, kc_ref[0], _NT, preferred_element_type=F32)
    s_scr[:, 0:t] = s_lat
    s_scr[:, t:] = s_ctx
    m_scr[...] = jnp.maximum(jnp.max(s_lat, axis=-1, keepdims=True), jnp.max(s_ctx, axis=-1, keepdims=True))


def _attention(lam, qkv, kvc, g_subln, bsz, t, n_ctx, out_scale):
    m, width = qkv.shape[1], qkv.shape[2]
    nh = width // V_DIM
    tq = _tile(t, 256)
    nq = t // tq
    s_all = t + n_ctx
    kern = functools.partial(_attn_kernel, tq=tq, out_scale=out_scale)
    return pl.pallas_call(
        kern,
        grid=(bsz, nh, nq + 1),
        in_specs=[pl.BlockSpec(memory_space=pltpu.SMEM),
                  pl.BlockSpec((1, tq, V_DIM), lambda b, h, j: (0, b * nq + jnp.minimum(j, nq - 1), h)),
                  pl.BlockSpec((1, t, V_DIM), lambda b, h, j: (1, b, h)),
                  pl.BlockSpec((1, t, V_DIM), lambda b, h, j: (2, b, h)),
                  pl.BlockSpec((1, n_ctx, V_DIM), lambda b, h, j: (0, b, h)),
                  pl.BlockSpec((1, n_ctx, V_DIM), lambda b, h, j: (1, b, h)),
                  pl.BlockSpec((1, V_DIM), lambda b, h, j: (0, 0))],
        out_specs=pl.BlockSpec((tq, V_DIM), lambda b, h, j: (b * nq + jnp.maximum(j - 1, 0), h)),
        out_shape=jax.ShapeDtypeStruct((m, width), BF16),
        scratch_shapes=[pltpu.VMEM((s_all, 2 * V_DIM), BF16),
                        pltpu.VMEM((2 * tq, s_all), F32),
                        pltpu.VMEM((2 * tq, 1), F32)],
        compiler_params=_cparams(("parallel", "parallel", "arbitrary")),
        name="attn",
    )(lam, qkv, qkv, qkv, kvc, kvc, g_subln)


def _post_kernel(o_ref, ga_ref, gy_ref, x_ref, mod_ref, wa_ref, wo_ref, gf_ref, wrt_ref,
                 xmid_ref, hf_ref, aff_ref):
    y_attn = jnp.dot(o_ref[...], wa_ref[...], preferred_element_type=F32)
    merged = ga_ref[0].astype(F32) * y_attn + gy_ref[...].astype(F32)
    mix = jnp.dot(merged.astype(BF16), wo_ref[...], preferred_element_type=F32)
    xm = x_ref[...] + mod_ref[0, 2:3, :] * mix
    xmid_ref[...] = xm
    ms = jnp.mean(xm * xm, axis=-1, keepdims=True)
    hf = xm * lax.rsqrt(ms + EPS) * gf_ref[...]
    hf = hf * (1.0 + mod_ref[0, 4:5, :]) + mod_ref[0, 3:4, :]
    hf_ref[...] = hf.astype(hf_ref.dtype)
    logits = _dot3(wrt_ref[...], hf, _NT)
    z = jnp.exp(logits - jnp.max(logits, axis=0, keepdims=True))
    aff_ref[0] = z / jnp.sum(z, axis=0, keepdims=True)


def _post(o, gates, gy, x2, mod3, wa_bf, wo_bf, gf, wrt, bsz, t):
    m, d = x2.shape
    aw = o.shape[1]
    ne = wrt.shape[0]
    tm = _tile(t, 512)
    nt = t // tm
    const = lambda shape: pl.BlockSpec(shape, lambda i: (0,) * len(shape))
    return pl.pallas_call(
        _post_kernel,
        grid=(m // tm,),
        in_specs=[pl.BlockSpec((tm, aw), lambda i: (i, 0)),
                  pl.BlockSpec((1, tm, d), lambda i: (1, i, 0)),
                  pl.BlockSpec((tm, d), lambda i: (i, 0)),
                  pl.BlockSpec((tm, d), lambda i: (i, 0)),
                  pl.BlockSpec((1, N_MOD, d), lambda i: (i // nt, 0, 0)),
                  const((aw, d)), const((d, d)), const((1, d)), const((ne, d))],
        out_specs=[pl.BlockSpec((tm, d), lambda i: (i, 0)),
                   pl.BlockSpec((tm, d), lambda i: (i, 0)),
                   pl.BlockSpec((1, ne, tm), lambda i: (i // nt, 0, i % nt))],
        out_shape=[jax.ShapeDtypeStruct((m, d), F32),
                   jax.ShapeDtypeStruct((m, d), BF16),
                   jax.ShapeDtypeStruct((bsz, ne, t), F32)],
        compiler_params=_cparams(("parallel",)),
        name="post",
    )(o, gates, gy, x2, mod3, wa_bf, wo_bf, gf, wrt)


def _count(mask):
    ones = jnp.where(mask, 1.0, 0.0)
    return jnp.sum(jnp.sum(ones, axis=2, keepdims=True), axis=1, keepdims=True)


def _route_kernel(a_ref, pos_ref, *, cap):
    a = a_ref[0]
    ne, nc, ln = a.shape
    rows = ne * nc
    bits = lax.bitcast_convert_type(a, jnp.int32)

    def search(i, cur):
        cand = cur | jnp.left_shift(jnp.int32(1), 30 - i)
        return jnp.where(_count(bits >= cand) >= cap, cand, cur)

    tau = lax.fori_loop(0, 31, search, jnp.zeros((ne, 1, 1), jnp.int32))

    r_i = lax.broadcasted_iota(jnp.int32, (rows, rows), 0)
    r_j = lax.broadcasted_iota(jnp.int32, (rows, rows), 1)
    shift = int(math.log2(nc))
    same_expert = lax.shift_right_logical(r_i, shift) == lax.shift_right_logical(r_j, shift)
    chunk_before = jnp.where(same_expert & (r_j < r_i), 1.0, 0.0).astype(BF16)
    l_i = lax.broadcasted_iota(jnp.int32, (ln, ln), 0)
    l_j = lax.broadcasted_iota(jnp.int32, (ln, ln), 1)
    lane_upto = jnp.where(l_i <= l_j, 1.0, 0.0).astype(BF16)

    def excl_prefix(mask):
        xm = jnp.where(mask, 1.0, 0.0).reshape(rows, ln)
        incl = jnp.dot(xm.astype(BF16), lane_upto, preferred_element_type=F32)
        tot = jnp.broadcast_to(incl[:, ln - 1:ln], (rows, ln))
        offs = jnp.dot(chunk_before, tot.astype(BF16), preferred_element_type=F32)
        return (incl + offs - xm).reshape(ne, nc, ln)

    gt = bits > tau
    eq = bits == tau
    need = cap - _count(gt)
    sel = gt | (eq & (excl_prefix(eq) < need))
    pos = excl_prefix(sel)
    pos_ref[0] = jnp.where(sel, pos, -1.0).astype(jnp.int32)


def _route(aff4, cap):
    bsz, ne, nc, ln = aff4.shape
    assert nc & (nc - 1) == 0, "token chunks per sample must be a power of two"
    return pl.pallas_call(
        functools.partial(_route_kernel, cap=cap),
        grid=(bsz,),
        in_specs=[pl.BlockSpec((1, ne, nc, ln), lambda b: (b, 0, 0, 0))],
        out_specs=pl.BlockSpec((1, ne, nc, ln), lambda b: (b, 0, 0, 0)),
        out_shape=jax.ShapeDtypeStruct((bsz, ne, nc, ln), jnp.int32),
        compiler_params=_cparams(("parallel",)),
        name="route",
    )(aff4)


def _gather_kernel(pos_ref, aff_ref, hf_ref, xs_ref, gate_ref, *, cap, tc):
    t = hf_ref.shape[1]
    d = hf_ref.shape[2]
    slot = lax.broadcasted_iota(jnp.int32, (cap, tc), 0)
    xs = jnp.zeros((cap, d), F32)
    gate = jnp.zeros((cap, 1), F32)
    for c in range(t // tc):
        cs = slice(c * tc, (c + 1) * tc)
        hit = pos_ref[0, 0, :, cs] == slot
        xs = xs + jnp.dot(jnp.where(hit, 1.0, 0.0).astype(BF16), hf_ref[0, cs, :], preferred_element_type=F32)
        gate = gate + jnp.sum(jnp.where(hit, aff_ref[0, 0, :, cs], 0.0), axis=1, keepdims=True)
    xs_ref[0, 0] = xs.astype(xs_ref.dtype)
    gate_ref[0, 0] = gate


def _gather(pos4, aff4, hf3, cap):
    bsz, ne, _, t = pos4.shape
    d = hf3.shape[2]
    kern = functools.partial(_gather_kernel, cap=cap, tc=_tile(t, 512))
    return pl.pallas_call(
        kern,
        grid=(bsz, ne),
        in_specs=[pl.BlockSpec((1, 1, 1, t), lambda b, e: (b, e, 0, 0)),
                  pl.BlockSpec((1, 1, 1, t), lambda b, e: (b, e, 0, 0)),
                  pl.BlockSpec((1, t, d), lambda b, e: (b, 0, 0))],
        out_specs=[pl.BlockSpec((1, 1, cap, d), lambda b, e: (e, b, 0, 0)),
                   pl.BlockSpec((1, 1, cap, 1), lambda b, e: (e, b, 0, 0))],
        out_shape=[jax.ShapeDtypeStruct((ne, bsz, cap, d), BF16),
                   jax.ShapeDtypeStruct((ne, bsz, cap, 1), F32)],
        compiler_params=_cparams(("parallel", "arbitrary")),
        name="gather",
    )(pos4, aff4, hf3)


def _expert_kernel(xs_ref, gate_ref, wg_ref, wu_ref, wd_ref, o_ref, acc):
    f = pl.program_id(2)

    @pl.when(f == 0)
    def _():
        acc[...] = jnp.zeros_like(acc)

    x = xs_ref[0]
    g = jnp.dot(x, wg_ref[0].astype(BF16), preferred_element_type=F32)
    u = jnp.dot(x, wu_ref[0].astype(BF16), preferred_element_type=F32)
    hid = (g * _sigmoid(g)) * u
    acc[...] += jnp.dot(hid.astype(BF16), wd_ref[0].astype(BF16), preferred_element_type=F32)

    @pl.when(f == pl.num_programs(2) - 1)
    def _():
        o_ref[0] = (acc[...] * gate_ref[0]).astype(o_ref.dtype)


def _experts(xs3, gate3, wg, wu, wd):
    ne, r, d = xs3.shape
    ff = wg.shape[2]
    tr = _tile(r, 2048)
    tf = _tile(ff, 256)
    return pl.pallas_call(
        _expert_kernel,
        grid=(ne, r // tr, ff // tf),
        in_specs=[pl.BlockSpec((1, tr, d), lambda e, i, f: (e, i, 0)),
                  pl.BlockSpec((1, tr, 1), lambda e, i, f: (e, i, 0)),
                  pl.BlockSpec((1, d, tf), lambda e, i, f: (e, 0, f)),
                  pl.BlockSpec((1, d, tf), lambda e, i, f: (e, 0, f)),
                  pl.BlockSpec((1, tf, d), lambda e, i, f: (e, f, 0))],
        out_specs=pl.BlockSpec((1, tr, d), lambda e, i, f: (e, i, 0)),
        out_shape=jax.ShapeDtypeStruct((ne, r, d), BF16),
        scratch_shapes=[pltpu.VMEM((tr, d), F32)],
        compiler_params=_cparams(("parallel", "parallel", "arbitrary")),
        name="experts",
    )(xs3, gate3, wg, wu, wd)


def _combine_kernel(pos_ref, ys_ref, xm_ref, mod_ref, gfin_ref, o_ref, acc, pos_t, *, cap):
    e = pl.program_id(2)
    ne, tt = pos_ref.shape[1], pos_ref.shape[2]

    @pl.when(e == 0)
    def _():
        acc[...] = jnp.zeros_like(acc)
        p = pos_ref[0].astype(F32)
        padded = jnp.concatenate([p, jnp.full((LANES - ne, tt), -1.0, F32)], axis=0)
        pos_t[...] = padded.T

    lane = lax.broadcasted_iota(jnp.int32, (tt, LANES), 1)
    col = jnp.sum(jnp.where(lane == e, pos_t[...], 0.0), axis=1, keepdims=True)
    slot = lax.broadcasted_iota(jnp.int32, (tt, cap), 1).astype(F32)
    onehot = jnp.where(col == slot, 1.0, 0.0).astype(BF16)
    acc[...] += jnp.dot(onehot, ys_ref[0, 0], preferred_element_type=F32)

    @pl.when(e == ne - 1)
    def _():
        xo = xm_ref[...] + mod_ref[0, 5:6, :] * acc[...]
        ms = jnp.mean(xo * xo, axis=-1, keepdims=True)
        o_ref[...] = xo * lax.rsqrt(ms + EPS) * gfin_ref[...]


def _combine(pos3, ys4, xmid, mod3, gfin, cap):
    bsz, ne, t = pos3.shape
    m, d = xmid.shape
    tt = _tile(t, 1024)
    nt = t // tt
    return pl.pallas_call(
        functools.partial(_combine_kernel, cap=cap),
        grid=(bsz, nt, ne),
        in_specs=[pl.BlockSpec((1, ne, tt), lambda b, i, e: (b, 0, i)),
                  pl.BlockSpec((1, 1, cap, d), lambda b, i, e: (e, b, 0, 0)),
                  pl.BlockSpec((tt, d), lambda b, i, e: (b * nt + i, 0)),
                  pl.BlockSpec((1, N_MOD, d), lambda b, i, e: (b, 0, 0)),
                  pl.BlockSpec((1, d), lambda b, i, e: (0, 0))],
        out_specs=pl.BlockSpec((tt, d), lambda b, i, e: (b * nt + i, 0)),
        out_shape=jax.ShapeDtypeStruct((m, d), F32),
        scratch_shapes=[pltpu.VMEM((tt, d), F32), pltpu.VMEM((tt, LANES), F32)],
        compiler_params=_cparams(("parallel", "parallel", "arbitrary")),
        name="combine",
    )(pos3, ys4, xmid, mod3, gfin)


def _rope_tables(t):
    n_freq = HEAD_DIM // 4
    tok = jnp.arange(t)
    inv = ROPE_THETA ** (-jnp.arange(n_freq, dtype=F32) / n_freq)
    ang_r = (tok // GRID_W)[:, None].astype(F32) * inv
    ang_c = (tok % GRID_W)[:, None].astype(F32) * inv
    cos64 = jnp.concatenate([jnp.cos(ang_r)] * 2 + [jnp.cos(ang_c)] * 2, axis=-1)
    sin64 = jnp.concatenate([-jnp.sin(ang_r), jnp.sin(ang_r), -jnp.sin(ang_c), jnp.sin(ang_c)], axis=-1)
    return jnp.tile(cos64, (1, 2)), jnp.tile(sin64, (1, 2))


def kernel(x, c, ctx, c_ctx, w_ada, b_ada, g_norm_mix, g_norm_ffn, w_in, w_dw, b_dw, ln_g_conv, ln_b_conv,
           w_conv_out, b_conv_out, lambda_q1, lambda_k1, lambda_q2, lambda_k2, g_subln, w_attn_out, w_out,
           w_router, w_expert_gate, w_expert_up, w_expert_down, g_final):
    bsz, t, d = x.shape
    n_ctx = ctx.shape[1]
    depth = w_ada.shape[0]
    assert depth == 1, "single-layer trunk: the context stream only feeds keys/values"
    ne = w_router.shape[2]
    cap = CAPACITY_FACTOR * t // ne
    dc = w_dw.shape[2]
    qkw = N_HEADS * 2 * HEAD_DIM
    aw = N_HEADS * V_DIM
    assert dc == d and qkw == d and aw == d, "projection groups are addressed as equal-width column blocks"
    layer = 0
    lam_init = 0.8 - 0.6 * math.exp(-0.3 * layer)
    lam = (jnp.exp(jnp.sum(lambda_q1[layer].astype(F32) * lambda_k1[layer].astype(F32)))
           - jnp.exp(jnp.sum(lambda_q2[layer].astype(F32) * lambda_k2[layer].astype(F32)))
           + lam_init).reshape(1)

    row = lambda v: v.reshape(1, -1)
    x2 = x.reshape(bsz * t, d)
    ctx2 = ctx.reshape(bsz * n_ctx, d)

    pad = (-(bsz + 1)) % 8
    cc = jnp.concatenate([c, c_ctx[None, :], jnp.zeros((pad, d), F32)], axis=0)
    mod = _adaln(cc, w_ada[layer], row(b_ada[layer]))
    mod_lat = mod[:bsz].reshape(bsz, N_MOD, d)
    mod_ctx = mod[bsz:bsz + 1].reshape(1, N_MOD, d)

    w_in_bf = w_in[layer].astype(BF16)
    g_mix = row(g_norm_mix[layer])
    cos, sin_signed = _rope_tables(t)

    u, qkv, gates = _proj_all(x2, mod_lat, g_mix, w_in_bf, cos, sin_signed, t)
    kvc = _proj_ctx(ctx2, mod_ctx, g_mix, w_in_bf, 3)

    gy = _conv_branch(u.reshape(bsz, t, dc), gates, w_dw[layer], row(b_dw[layer]), row(ln_g_conv[layer]),
                      row(ln_b_conv[layer]), w_conv_out[layer].astype(BF16), row(b_conv_out[layer]))
    o = _attention(lam, qkv, kvc, row(g_subln[layer]), bsz, t, n_ctx, 1.0 - lam_init)

    xmid, hf, aff_t = _post(o, gates, gy, x2, mod_lat, w_attn_out[layer].astype(BF16), w_out[layer].astype(BF16),
                            row(g_norm_ffn[layer]), w_router[layer].T, bsz, t)

    pos = _route(aff_t.reshape(bsz, ne, t // LANES, LANES), cap)
    xs, gate = _gather(pos.reshape(bsz, ne, 1, t), aff_t.reshape(bsz, ne, 1, t), hf.reshape(bsz, t, d), cap)
    ys = _experts(xs.reshape(ne, bsz * cap, d), gate.reshape(ne, bsz * cap, 1),
                  w_expert_gate[layer], w_expert_up[layer], w_expert_down[layer])
    out = _combine(pos.reshape(bsz, ne, t), ys.reshape(ne, bsz, cap, d), xmid, mod_lat, row(g_final), cap)
    return out.reshape(bsz, t, d)
```

```python
import functools
import math

import jax
import jax.numpy as jnp
from jax import lax
from jax.experimental import pallas as pl
from jax.experimental.pallas import tpu as pltpu

N_HEADS = 8
HEAD_DIM = 64
V_DIM = 2 * HEAD_DIM
CONV_WIDTH = 31
CONV_HALF = CONV_WIDTH // 2
N_MOD = 6
GRID_W = 64
ROPE_THETA = 10000.0
CAPACITY_FACTOR = 2
EPS = 1e-6

LANES = 128
SUBLANES = 8
HALO = 16
VMEM_LIMIT = 56 * 1024 * 1024

F32 = jnp.float32
BF16 = jnp.bfloat16


def _cparams(sem):
    return pltpu.CompilerParams(dimension_semantics=sem, vmem_limit_bytes=VMEM_LIMIT)


def _tile(n, pref):
    t = min(n, pref)
    while n % t:
        t //= 2
    return t


def _split_bf16(a):
    hi = a.astype(BF16)
    lo = (a - hi.astype(F32)).astype(BF16)
    return hi, lo


def _dot3(a, b, dims):
    ah, al = _split_bf16(a)
    bh, bl = _split_bf16(b)
    dg = functools.partial(lax.dot_general, dimension_numbers=dims, preferred_element_type=F32)
    return dg(ah, bh) + (dg(ah, bl) + dg(al, bh))


_NN = (((1,), (0,)), ((), ()))
_NT = (((1,), (1,)), ((), ()))


def _sigmoid(x):
    return 1.0 / (1.0 + jnp.exp(-x))


def _adaln_kernel(c_ref, w_ref, b_ref, o_ref):
    c = c_ref[...]
    s = c * _sigmoid(c)
    o_ref[...] = _dot3(s, w_ref[...], _NN) + b_ref[...]


def _adaln(cc, w, b):
    rows, d = cc.shape
    n = w.shape[1]
    tn = _tile(n, 1024)
    return pl.pallas_call(
        _adaln_kernel,
        grid=(n // tn,),
        in_specs=[pl.BlockSpec((rows, d), lambda j: (0, 0)),
                  pl.BlockSpec((d, tn), lambda j: (0, j)),
                  pl.BlockSpec((1, tn), lambda j: (0, j))],
        out_specs=pl.BlockSpec((rows, tn), lambda j: (0, j)),
        out_shape=jax.ShapeDtypeStruct((rows, n), F32),
        compiler_params=_cparams(("arbitrary",)),
        name="adaln",
    )(cc, w, b)


def _norm_mod(x_ref, mod_ref, g_ref):
    x = x_ref[...]
    ms = jnp.mean(x * x, axis=-1, keepdims=True)
    y = x * lax.rsqrt(ms + EPS) * g_ref[...]
    return (y * (1.0 + mod_ref[0, 1:2, :]) + mod_ref[0, 0:1, :]).astype(BF16)


def _rope(acc, cos, sin_signed):
    lane = lax.broadcasted_iota(jnp.int32, (acc.shape[0], LANES), 1)
    first_half = (lane % 32) < 16
    outs = []
    for hh in range(acc.shape[1] // LANES):
        seg = acc[:, hh * LANES:(hh + 1) * LANES]
        partner = jnp.where(first_half, pltpu.roll(seg, LANES - 16, 1), pltpu.roll(seg, 16, 1))
        outs.append(seg * cos + partner * sin_signed)
    return jnp.concatenate(outs, axis=1)


def _proj_ctx_kernel(x_ref, mod_ref, g_ref, wk_ref, wv_ref, o_ref):
    h = _norm_mod(x_ref, mod_ref, g_ref)
    o_ref[0] = jnp.dot(h, wk_ref[...], preferred_element_type=F32).astype(o_ref.dtype)
    o_ref[1] = jnp.dot(h, wv_ref[...], preferred_element_type=F32).astype(o_ref.dtype)


def _proj_ctx(x2, mod3, g, w_bf, k_group):
    m, d = x2.shape
    tm = _tile(m, 512)
    return pl.pallas_call(
        _proj_ctx_kernel,
        grid=(m // tm,),
        in_specs=[pl.BlockSpec((tm, d), lambda i: (i, 0)),
                  pl.BlockSpec((1, N_MOD, d), lambda i: (0, 0, 0)),
                  pl.BlockSpec((1, d), lambda i: (0, 0)),
                  pl.BlockSpec((d, d), lambda i: (0, k_group)),
                  pl.BlockSpec((d, d), lambda i: (0, k_group + 1))],
        out_specs=pl.BlockSpec((2, tm, d), lambda i: (0, i, 0)),
        out_shape=jax.ShapeDtypeStruct((2, m, d), BF16),
        compiler_params=_cparams(("parallel",)),
        name="proj_ctx",
    )(x2, mod3, g, w_bf, w_bf)


def _proj_all_kernel(x_ref, mod_ref, g_ref, w_ref, cos_ref, sin_ref, u_ref, qkv_ref, gate_ref):
    d = x_ref.shape[1]
    h = _norm_mod(x_ref, mod_ref, g_ref)

    def group(n):
        return jnp.dot(h, w_ref[:, n * d:(n + 1) * d], preferred_element_type=F32)

    cos = cos_ref[...]
    sin = sin_ref[...]
    u_ref[...] = (group(0) * _sigmoid(group(1))).astype(u_ref.dtype)
    qkv_ref[0] = _rope(group(2) * (HEAD_DIM ** -0.5 * math.log2(math.e)), cos, sin).astype(qkv_ref.dtype)
    qkv_ref[1] = _rope(group(3), cos, sin).astype(qkv_ref.dtype)
    qkv_ref[2] = group(4).astype(qkv_ref.dtype)
    gate_ref[0] = _sigmoid(group(5)).astype(gate_ref.dtype)
    gate_ref[1] = _sigmoid(group(6)).astype(gate_ref.dtype)


def _proj_all(x2, mod3, g, w_bf, cos, sin_signed, t):
    m, d = x2.shape
    tm = _tile(t, 512)
    tblocks = t // tm
    return pl.pallas_call(
        _proj_all_kernel,
        grid=(m // tm,),
        in_specs=[pl.BlockSpec((tm, d), lambda i: (i, 0)),
                  pl.BlockSpec((1, N_MOD, d), lambda i: (i // tblocks, 0, 0)),
                  pl.BlockSpec((1, d), lambda i: (0, 0)),
                  pl.BlockSpec(w_bf.shape, lambda i: (0, 0), pipeline_mode=pl.Buffered(1)),
                  pl.BlockSpec((tm, LANES), lambda i: (i % tblocks, 0)),
                  pl.BlockSpec((tm, LANES), lambda i: (i % tblocks, 0))],
        out_specs=[pl.BlockSpec((tm, d), lambda i: (i, 0)),
                   pl.BlockSpec((3, tm, d), lambda i: (0, i, 0)),
                   pl.BlockSpec((2, tm, d), lambda i: (0, i, 0))],
        out_shape=[jax.ShapeDtypeStruct((m, d), BF16),
                   jax.ShapeDtypeStruct((3, m, d), BF16),
                   jax.ShapeDtypeStruct((2, m, d), BF16)],
        compiler_params=_cparams(("parallel",)),
        name="proj_all",
    )(x2, mod3, g, w_bf, cos, sin_signed)


def _conv_kernel(up_ref, um_ref, un_ref, gc_ref, wdw_ref, bdw_ref, lng_ref, lnb_ref, wc_ref, bc_ref,
                 o_ref, buf, cv, *, tt, rb):
    ti = pl.program_id(1)
    nt = pl.num_programs(1)
    d = um_ref.shape[2]
    prev = up_ref[0].astype(F32)
    nxt = un_ref[0].astype(F32)
    buf[0:HALO, :] = jnp.where(ti > 0, prev, 0.0)
    buf[HALO:HALO + tt, :] = um_ref[0].astype(F32)
    buf[HALO + tt:2 * HALO + tt, :] = jnp.where(ti < nt - 1, nxt, 0.0)

    off = HALO - CONV_HALF
    taps = [[(s - off, s // SUBLANES) for s in range(off, off + CONV_WIDTH) if s % SUBLANES == r]
            for r in range(SUBLANES)]
    n_slab = (off + CONV_WIDTH - 1) // SUBLANES + 1
    for c in range(d // LANES):
        cs = slice(c * LANES, (c + 1) * LANES)
        for rblk in range(tt // rb):
            r0 = rblk * rb
            slabs = [buf[r0 + SUBLANES * a:r0 + SUBLANES * a + rb + SUBLANES, cs] for a in range(n_slab)]
            acc = None
            for r in range(SUBLANES):
                z = None
                for k, a in taps[r]:
                    term = slabs[a] * wdw_ref[k:k + 1, cs]
                    z = term if z is None else z + term
                z = z[r:r + rb]
                acc = z if acc is None else acc + z
            cv[r0:r0 + rb, cs] = acc

    v = cv[...] + bdw_ref[...]
    mu = jnp.mean(v, axis=-1, keepdims=True)
    xc = v - mu
    var = jnp.mean(xc * xc, axis=-1, keepdims=True)
    y = xc * lax.rsqrt(var + EPS) * lng_ref[...] + lnb_ref[...]
    z = y * _sigmoid(y)
    yc = jnp.dot(z.astype(BF16), wc_ref[...], preferred_element_type=F32) + bc_ref[...]
    o_ref[...] = (gc_ref[0].astype(F32) * yc).astype(o_ref.dtype)


def _conv_branch(u3, gates, wdw, bdw, lng, lnb, wc_bf, bc):
    bsz, t, dc = u3.shape
    d = wc_bf.shape[1]
    tt = _tile(t, 256)
    nt = t // tt
    hb = tt // HALO
    nhb = t // HALO
    kern = functools.partial(_conv_kernel, tt=tt, rb=_tile(tt, 64))
    const = lambda shape: pl.BlockSpec(shape, lambda b, i: (0,) * len(shape))
    return pl.pallas_call(
        kern,
        grid=(bsz, nt),
        in_specs=[pl.BlockSpec((1, HALO, dc), lambda b, i: (b, jnp.maximum(i * hb - 1, 0), 0)),
                  pl.BlockSpec((1, tt, dc), lambda b, i: (b, i, 0)),
                  pl.BlockSpec((1, HALO, dc), lambda b, i: (b, jnp.minimum((i + 1) * hb, nhb - 1), 0)),
                  pl.BlockSpec((1, tt, d), lambda b, i: (0, b * nt + i, 0)),
                  const((CONV_WIDTH, dc)), const((1, dc)), const((1, dc)), const((1, dc)),
                  const((dc, d)), const((1, d))],
        out_specs=pl.BlockSpec((tt, d), lambda b, i: (b * nt + i, 0)),
        out_shape=jax.ShapeDtypeStruct((bsz * t, d), BF16),
        scratch_shapes=[pltpu.VMEM((tt + 2 * HALO, dc), F32), pltpu.VMEM((tt, dc), F32)],
        compiler_params=_cparams(("parallel", "parallel")),
        name="conv",
    )(u3, u3, u3, gates, wdw, bdw, lng, lnb, wc_bf, bc)


def _attn_kernel(lam_ref, q_ref, kl_ref, vl_ref, kc_ref, vc_ref, gs_ref, o_ref, vext, s_scr, m_scr,
                 *, tq, nq, n_tiles, out_scale):
    n = pl.program_id(0)
    t = kl_ref.shape[1]
    pair = lax.div(n, nq)
    prev_pair = lax.div(jnp.maximum(n - 1, 0), nq)

    @pl.when((lax.rem(n, nq) == 0) & (n < n_tiles))
    def _():
        slot = lax.rem(pair, 2)
        vext[slot, 0:t, 0:V_DIM] = vl_ref[0]
        vext[slot, t:, 0:V_DIM] = vc_ref[0]
        vext[slot, :, V_DIM:] = jnp.ones((vext.shape[1], V_DIM), vext.dtype)

    @pl.when(n == 0)
    def _():
        s_scr[...] = jnp.zeros_like(s_scr)
        m_scr[...] = jnp.zeros_like(m_scr)

    p = jnp.exp2(s_scr[...] - m_scr[...]).astype(BF16)
    acc = jnp.dot(p, vext[lax.rem(prev_pair, 2)], preferred_element_type=F32)
    o = acc[:, :V_DIM] / acc[:, V_DIM:]
    o = o[:tq] - lam_ref[0] * o[tq:]
    ms = jnp.mean(o * o, axis=-1, keepdims=True)
    o_ref[...] = (o * lax.rsqrt(ms + EPS) * gs_ref[...] * out_scale).astype(o_ref.dtype)

    q = q_ref[0]
    lane = lax.broadcasted_iota(jnp.int32, q.shape, 1)
    zero = jnp.zeros_like(q)
    qs = jnp.concatenate([jnp.where(lane < HEAD_DIM, q, zero), jnp.where(lane >= HEAD_DIM, q, zero)], axis=0)
    s_lat = lax.dot_general(qs, kl_ref[0], _NT, preferred_element_type=F32)
    s_ctx = lax.dot_general(qs, kc_ref[0], _NT, preferred_element_type=F32)
    s_scr[:, 0:t] = s_lat
    s_scr[:, t:] = s_ctx
    m_scr[...] = jnp.maximum(jnp.max(s_lat, axis=-1, keepdims=True), jnp.max(s_ctx, axis=-1, keepdims=True))


def _attention(lam, qkv, kvc, g_subln, bsz, t, n_ctx, out_scale):
    m, width = qkv.shape[1], qkv.shape[2]
    nh = width // V_DIM
    tq = _tile(t, 512)
    nq = t // tq
    s_all = t + n_ctx
    n_tiles = bsz * nh * nq
    kern = functools.partial(_attn_kernel, tq=tq, nq=nq, n_tiles=n_tiles, out_scale=out_scale)

    def tile_coords(n):
        pair = lax.div(n, nq)
        return lax.div(pair, nh), lax.rem(pair, nh), lax.rem(n, nq)

    def started(n):
        return tile_coords(jnp.minimum(n, n_tiles - 1))

    def finished(n):
        return tile_coords(jnp.maximum(n - 1, 0))

    def q_map(n):
        b, h, j = started(n)
        return 0, b * nq + j, h

    def kv_map(which):
        def index(n):
            b, h, _ = started(n)
            return which, b, h
        return index

    def out_map(n):
        b, h, j = finished(n)
        return b * nq + j, h

    return pl.pallas_call(
        kern,
        grid=(n_tiles + 1,),
        in_specs=[pl.BlockSpec(memory_space=pltpu.SMEM),
                  pl.BlockSpec((1, tq, V_DIM), q_map),
                  pl.BlockSpec((1, t, V_DIM), kv_map(1)),
                  pl.BlockSpec((1, t, V_DIM), kv_map(2)),
                  pl.BlockSpec((1, n_ctx, V_DIM), kv_map(0)),
                  pl.BlockSpec((1, n_ctx, V_DIM), kv_map(1)),
                  pl.BlockSpec((1, V_DIM), lambda n: (0, 0))],
        out_specs=pl.BlockSpec((tq, V_DIM), out_map),
        out_shape=jax.ShapeDtypeStruct((m, width), BF16),
        scratch_shapes=[pltpu.VMEM((2, s_all, 2 * V_DIM), BF16),
                        pltpu.VMEM((2 * tq, s_all), F32),
                        pltpu.VMEM((2 * tq, 1), F32)],
        compiler_params=_cparams(("arbitrary",)),
        name="attn",
    )(lam, qkv, qkv, qkv, kvc, kvc, g_subln)


def _post_kernel(o_ref, ga_ref, gy_ref, x_ref, mod_ref, wa_ref, wo_ref, gf_ref, wrt_ref,
                 xmid_ref, hf_ref, aff_ref):
    y_attn = jnp.dot(o_ref[...], wa_ref[...], preferred_element_type=F32)
    merged = ga_ref[0].astype(F32) * y_attn + gy_ref[...].astype(F32)
    mix = jnp.dot(merged.astype(BF16), wo_ref[...], preferred_element_type=F32)
    xm = x_ref[...] + mod_ref[0, 2:3, :] * mix
    xmid_ref[...] = xm
    ms = jnp.mean(xm * xm, axis=-1, keepdims=True)
    hf = xm * lax.rsqrt(ms + EPS) * gf_ref[...]
    hf = hf * (1.0 + mod_ref[0, 4:5, :]) + mod_ref[0, 3:4, :]
    hf_ref[...] = hf.astype(hf_ref.dtype)
    logits = _dot3(wrt_ref[...], hf, _NT)
    z = jnp.exp(logits - jnp.max(logits, axis=0, keepdims=True))
    aff_ref[0] = z / jnp.sum(z, axis=0, keepdims=True)


def _post(o, gates, gy, x2, mod3, wa_bf, wo_bf, gf, wrt, bsz, t):
    m, d = x2.shape
    aw = o.shape[1]
    ne = wrt.shape[0]
    tm = _tile(t, 512)
    nt = t // tm
    const = lambda shape: pl.BlockSpec(shape, lambda i: (0,) * len(shape))
    return pl.pallas_call(
        _post_kernel,
        grid=(m // tm,),
        in_specs=[pl.BlockSpec((tm, aw), lambda i: (i, 0)),
                  pl.BlockSpec((1, tm, d), lambda i: (1, i, 0)),
                  pl.BlockSpec((tm, d), lambda i: (i, 0)),
                  pl.BlockSpec((tm, d), lambda i: (i, 0)),
                  pl.BlockSpec((1, N_MOD, d), lambda i: (i // nt, 0, 0)),
                  const((aw, d)), const((d, d)), const((1, d)), const((ne, d))],
        out_specs=[pl.BlockSpec((tm, d), lambda i: (i, 0)),
                   pl.BlockSpec((tm, d), lambda i: (i, 0)),
                   pl.BlockSpec((1, ne, tm), lambda i: (i // nt, 0, i % nt))],
        out_shape=[jax.ShapeDtypeStruct((m, d), F32),
                   jax.ShapeDtypeStruct((m, d), BF16),
                   jax.ShapeDtypeStruct((bsz, ne, t), F32)],
        compiler_params=_cparams(("parallel",)),
        name="post",
    )(o, gates, gy, x2, mod3, wa_bf, wo_bf, gf, wrt)


def _count(mask):
    ones = jnp.where(mask, 1.0, 0.0)
    return jnp.sum(jnp.sum(ones, axis=2, keepdims=True), axis=1, keepdims=True)


def _route_kernel(a_ref, pos_ref, *, cap):
    a = a_ref[0]
    ne, nc, ln = a.shape
    rows = ne * nc
    bits = lax.bitcast_convert_type(a, jnp.int32)

    def search(i, cur):
        cand = cur | jnp.left_shift(jnp.int32(1), 30 - i)
        return jnp.where(_count(bits >= cand) >= cap, cand, cur)

    tau = lax.fori_loop(0, 31, search, jnp.zeros((ne, 1, 1), jnp.int32))

    r_i = lax.broadcasted_iota(jnp.int32, (rows, rows), 0)
    r_j = lax.broadcasted_iota(jnp.int32, (rows, rows), 1)
    shift = int(math.log2(nc))
    same_expert = lax.shift_right_logical(r_i, shift) == lax.shift_right_logical(r_j, shift)
    chunk_before = jnp.where(same_expert & (r_j < r_i), 1.0, 0.0).astype(BF16)
    l_i = lax.broadcasted_iota(jnp.int32, (ln, ln), 0)
    l_j = lax.broadcasted_iota(jnp.int32, (ln, ln), 1)
    lane_upto = jnp.where(l_i <= l_j, 1.0, 0.0).astype(BF16)

    def excl_prefix(mask):
        xm = jnp.where(mask, 1.0, 0.0).reshape(rows, ln)
        incl = jnp.dot(xm.astype(BF16), lane_upto, preferred_element_type=F32)
        tot = jnp.broadcast_to(incl[:, ln - 1:ln], (rows, ln))
        offs = jnp.dot(chunk_before, tot.astype(BF16), preferred_element_type=F32)
        return (incl + offs - xm).reshape(ne, nc, ln)

    gt = bits > tau
    eq = bits == tau
    need = cap - _count(gt)
    sel = gt | (eq & (excl_prefix(eq) < need))
    pos = excl_prefix(sel)
    pos_ref[0] = jnp.where(sel, pos, -1.0).astype(jnp.int32)


def _route(aff4, cap):
    bsz, ne, nc, ln = aff4.shape
    assert nc & (nc - 1) == 0, "token chunks per sample must be a power of two"
    spec = pl.BlockSpec((1, ne, nc, ln), lambda b: (b, 0, 0, 0))
    return pl.pallas_call(
        functools.partial(_route_kernel, cap=cap),
        grid=(bsz,),
        in_specs=[spec],
        out_specs=spec,
        out_shape=jax.ShapeDtypeStruct((bsz, ne, nc, ln), jnp.int32),
        compiler_params=_cparams(("parallel",)),
        name="route",
    )(aff4)


def _gather_kernel(pos_ref, aff_ref, hf_ref, xs_ref, gate_ref, *, cap, tc):
    t = hf_ref.shape[1]
    d = hf_ref.shape[2]
    slot = lax.broadcasted_iota(jnp.int32, (cap, tc), 0)
    xs = jnp.zeros((cap, d), F32)
    gate = jnp.zeros((cap, 1), F32)
    for c in range(t // tc):
        cs = slice(c * tc, (c + 1) * tc)
        hit = pos_ref[0, 0, :, cs] == slot
        xs = xs + jnp.dot(jnp.where(hit, 1.0, 0.0).astype(BF16), hf_ref[0, cs, :], preferred_element_type=F32)
        gate = gate + jnp.sum(jnp.where(hit, aff_ref[0, 0, :, cs], 0.0), axis=1, keepdims=True)
    xs_ref[0, 0] = xs.astype(xs_ref.dtype)
    gate_ref[0, 0] = gate


def _gather(pos4, aff4, hf3, cap):
    bsz, ne, _, t = pos4.shape
    d = hf3.shape[2]
    kern = functools.partial(_gather_kernel, cap=cap, tc=_tile(t, 512))
    return pl.pallas_call(
        kern,
        grid=(bsz, ne),
        in_specs=[pl.BlockSpec((1, 1, 1, t), lambda b, e: (b, e, 0, 0)),
                  pl.BlockSpec((1, 1, 1, t), lambda b, e: (b, e, 0, 0)),
                  pl.BlockSpec((1, t, d), lambda b, e: (b, 0, 0))],
        out_specs=[pl.BlockSpec((1, 1, cap, d), lambda b, e: (e, b, 0, 0)),
                   pl.BlockSpec((1, 1, cap, 1), lambda b, e: (e, b, 0, 0))],
        out_shape=[jax.ShapeDtypeStruct((ne, bsz, cap, d), BF16),
                   jax.ShapeDtypeStruct((ne, bsz, cap, 1), F32)],
        compiler_params=_cparams(("parallel", "arbitrary")),
        name="gather",
    )(pos4, aff4, hf3)


def _expert_kernel(xs_ref, gate_ref, wg_ref, wu_ref, wd_ref, o_ref, acc):
    f = pl.program_id(2)

    @pl.when(f == 0)
    def _():
        acc[...] = jnp.zeros_like(acc)

    x = xs_ref[0]
    g = jnp.dot(x, wg_ref[0].astype(BF16), preferred_element_type=F32)
    u = jnp.dot(x, wu_ref[0].astype(BF16), preferred_element_type=F32)
    hid = (g * _sigmoid(g)) * u
    acc[...] += jnp.dot(hid.astype(BF16), wd_ref[0].astype(BF16), preferred_element_type=F32)

    @pl.when(f == pl.num_programs(2) - 1)
    def _():
        o_ref[0] = (acc[...] * gate_ref[0]).astype(o_ref.dtype)


def _experts(xs3, gate3, wg, wu, wd):
    ne, r, d = xs3.shape
    ff = wg.shape[2]
    tr = _tile(r, 2048)
    tf = _tile(ff, 256)
    return pl.pallas_call(
        _expert_kernel,
        grid=(ne, r // tr, ff // tf),
        in_specs=[pl.BlockSpec((1, tr, d), lambda e, i, f: (e, i, 0)),
                  pl.BlockSpec((1, tr, 1), lambda e, i, f: (e, i, 0)),
                  pl.BlockSpec((1, d, tf), lambda e, i, f: (e, 0, f)),
                  pl.BlockSpec((1, d, tf), lambda e, i, f: (e, 0, f)),
                  pl.BlockSpec((1, tf, d), lambda e, i, f: (e, f, 0))],
        out_specs=pl.BlockSpec((1, tr, d), lambda e, i, f: (e, i, 0)),
        out_shape=jax.ShapeDtypeStruct((ne, r, d), BF16),
        scratch_shapes=[pltpu.VMEM((tr, d), F32)],
        compiler_params=_cparams(("parallel", "parallel", "arbitrary")),
        name="experts",
    )(xs3, gate3, wg, wu, wd)


def _combine_kernel(pos_ref, ys_ref, xm_ref, mod_ref, gfin_ref, o_ref, *, cap):
    ne, tt = pos_ref.shape[1], pos_ref.shape[2]
    p = pos_ref[0].astype(F32)
    padded = jnp.concatenate([p, jnp.full((LANES - ne, tt), -1.0, F32)], axis=0)
    pos_t = padded.T
    slot = lax.broadcasted_iota(jnp.int32, (tt, cap), 1).astype(F32)
    acc = None
    for e in range(ne):
        onehot = jnp.where(pos_t[:, e:e + 1] == slot, 1.0, 0.0).astype(BF16)
        part = jnp.dot(onehot, ys_ref[e, 0], preferred_element_type=F32)
        acc = part if acc is None else acc + part
    xo = xm_ref[...] + mod_ref[0, 5:6, :] * acc
    ms = jnp.mean(xo * xo, axis=-1, keepdims=True)
    o_ref[...] = xo * lax.rsqrt(ms + EPS) * gfin_ref[...]


def _combine(pos3, ys4, xmid, mod3, gfin, cap):
    bsz, ne, t = pos3.shape
    m, d = xmid.shape
    tt = _tile(t, 512)
    nt = t // tt
    return pl.pallas_call(
        functools.partial(_combine_kernel, cap=cap),
        grid=(bsz, nt),
        in_specs=[pl.BlockSpec((1, ne, tt), lambda b, i: (b, 0, i)),
                  pl.BlockSpec((ne, 1, cap, d), lambda b, i: (0, b, 0, 0), pipeline_mode=pl.Buffered(1)),
                  pl.BlockSpec((tt, d), lambda b, i: (b * nt + i, 0)),
                  pl.BlockSpec((1, N_MOD, d), lambda b, i: (b, 0, 0)),
                  pl.BlockSpec((1, d), lambda b, i: (0, 0))],
        out_specs=pl.BlockSpec((tt, d), lambda b, i: (b * nt + i, 0)),
        out_shape=jax.ShapeDtypeStruct((m, d), F32),
        compiler_params=_cparams(("parallel", "arbitrary")),
        name="combine",
    )(pos3, ys4, xmid, mod3, gfin)


def _rope_tables(t):
    n_freq = HEAD_DIM // 4
    tok = jnp.arange(t)
    inv = ROPE_THETA ** (-jnp.arange(n_freq, dtype=F32) / n_freq)
    ang_r = (tok // GRID_W)[:, None].astype(F32) * inv
    ang_c = (tok % GRID_W)[:, None].astype(F32) * inv
    cos64 = jnp.concatenate([jnp.cos(ang_r)] * 2 + [jnp.cos(ang_c)] * 2, axis=-1)
    sin64 = jnp.concatenate([-jnp.sin(ang_r), jnp.sin(ang_r), -jnp.sin(ang_c), jnp.sin(ang_c)], axis=-1)
    return jnp.tile(cos64, (1, 2)), jnp.tile(sin64, (1, 2))


def kernel(x, c, ctx, c_ctx, w_ada, b_ada, g_norm_mix, g_norm_ffn, w_in, w_dw, b_dw, ln_g_conv, ln_b_conv,
           w_conv_out, b_conv_out, lambda_q1, lambda_k1, lambda_q2, lambda_k2, g_subln, w_attn_out, w_out,
           w_router, w_expert_gate, w_expert_up, w_expert_down, g_final):
    bsz, t, d = x.shape
    n_ctx = ctx.shape[1]
    depth = w_ada.shape[0]
    assert depth == 1, "single-layer trunk: the context stream only feeds keys/values"
    ne = w_router.shape[2]
    cap = CAPACITY_FACTOR * t // ne
    dc = w_dw.shape[2]
    qkw = N_HEADS * 2 * HEAD_DIM
    aw = N_HEADS * V_DIM
    assert dc == d and qkw == d and aw == d, "projection groups are addressed as equal-width column blocks"
    layer = 0
    lam_init = 0.8 - 0.6 * math.exp(-0.3 * layer)
    lam = (jnp.exp(jnp.sum(lambda_q1[layer].astype(F32) * lambda_k1[layer].astype(F32)))
           - jnp.exp(jnp.sum(lambda_q2[layer].astype(F32) * lambda_k2[layer].astype(F32)))
           + lam_init).reshape(1)

    row = lambda v: v.reshape(1, -1)
    x2 = x.reshape(bsz * t, d)
    ctx2 = ctx.reshape(bsz * n_ctx, d)

    pad = (-(bsz + 1)) % 8
    cc = jnp.concatenate([c, c_ctx[None, :], jnp.zeros((pad, d), F32)], axis=0)
    mod = _adaln(cc, w_ada[layer], row(b_ada[layer]))
    mod_lat = mod[:bsz].reshape(bsz, N_MOD, d)
    mod_ctx = mod[bsz:bsz + 1].reshape(1, N_MOD, d)

    w_in_bf = w_in[layer].astype(BF16)
    g_mix = row(g_norm_mix[layer])
    cos, sin_signed = _rope_tables(t)

    u, qkv, gates = _proj_all(x2, mod_lat, g_mix, w_in_bf, cos, sin_signed, t)
    kvc = _proj_ctx(ctx2, mod_ctx, g_mix, w_in_bf, 3)

    gy = _conv_branch(u.reshape(bsz, t, dc), gates, w_dw[layer], row(b_dw[layer]), row(ln_g_conv[layer]),
                      row(ln_b_conv[layer]), w_conv_out[layer].astype(BF16), row(b_conv_out[layer]))
    o = _attention(lam, qkv, kvc, row(g_subln[layer]), bsz, t, n_ctx, 1.0 - lam_init)

    xmid, hf, aff_t = _post(o, gates, gy, x2, mod_lat, w_attn_out[layer].astype(BF16), w_out[layer].astype(BF16),
                            row(g_norm_ffn[layer]), w_router[layer].T, bsz, t)

    pos = _route(aff_t.reshape(bsz, ne, t // LANES, LANES), cap)
    xs, gate = _gather(pos.reshape(bsz, ne, 1, t), aff_t.reshape(bsz, ne, 1, t), hf.reshape(bsz, t, d), cap)
    ys = _experts(xs.reshape(ne, bsz * cap, d), gate.reshape(ne, bsz * cap, 1),
                  w_expert_gate[layer], w_expert_up[layer], w_expert_down[layer])
    out = _combine(pos.reshape(bsz, ne, t), ys.reshape(ne, bsz, cap, d), xmid, mod_lat, row(g_final), cap)
    return out.reshape(bsz, t, d)
```

```python
import functools
import math

import jax
import jax.numpy as jnp
from jax import lax
from jax.experimental import pallas as pl
from jax.experimental.pallas import tpu as pltpu

N_HEADS = 8
HEAD_DIM = 64
V_DIM = 2 * HEAD_DIM
CONV_WIDTH = 31
CONV_HALF = CONV_WIDTH // 2
N_MOD = 6
GRID_W = 64
ROPE_THETA = 10000.0
CAPACITY_FACTOR = 2
EPS = 1e-6

LANES = 128
SUBLANES = 8
HALO = 16
VMEM_LIMIT = 56 * 1024 * 1024

F32 = jnp.float32
BF16 = jnp.bfloat16


def _cparams(sem):
    return pltpu.CompilerParams(dimension_semantics=sem, vmem_limit_bytes=VMEM_LIMIT)


def _tile(n, pref):
    t = min(n, pref)
    while n % t:
        t //= 2
    return t


def _split_bf16(a):
    hi = a.astype(BF16)
    lo = (a - hi.astype(F32)).astype(BF16)
    return hi, lo


def _dot3(a, b, dims):
    ah, al = _split_bf16(a)
    bh, bl = _split_bf16(b)
    dg = functools.partial(lax.dot_general, dimension_numbers=dims, preferred_element_type=F32)
    return dg(ah, bh) + (dg(ah, bl) + dg(al, bh))


_NN = (((1,), (0,)), ((), ()))
_NT = (((1,), (1,)), ((), ()))


def _sigmoid(x):
    return 1.0 / (1.0 + jnp.exp(-x))


def _adaln_kernel(c_ref, w_ref, b_ref, o_ref):
    c = c_ref[...]
    s = c * _sigmoid(c)
    o_ref[...] = _dot3(s, w_ref[...], _NN) + b_ref[...]


def _adaln(cc, w, b):
    rows, d = cc.shape
    n = w.shape[1]
    tn = _tile(n, 1024)
    return pl.pallas_call(
        _adaln_kernel,
        grid=(n // tn,),
        in_specs=[pl.BlockSpec((rows, d), lambda j: (0, 0)),
                  pl.BlockSpec((d, tn), lambda j: (0, j)),
                  pl.BlockSpec((1, tn), lambda j: (0, j))],
        out_specs=pl.BlockSpec((rows, tn), lambda j: (0, j)),
        out_shape=jax.ShapeDtypeStruct((rows, n), F32),
        compiler_params=_cparams(("arbitrary",)),
        name="adaln",
    )(cc, w, b)


def _norm_mod(x_ref, mod_ref, g_ref):
    x = x_ref[...]
    ms = jnp.mean(x * x, axis=-1, keepdims=True)
    y = x * lax.rsqrt(ms + EPS) * g_ref[...]
    return (y * (1.0 + mod_ref[0, 1:2, :]) + mod_ref[0, 0:1, :]).astype(BF16)


def _rope(acc, cos, sin_signed):
    lane = lax.broadcasted_iota(jnp.int32, (acc.shape[0], LANES), 1)
    first_half = (lane % 32) < 16
    outs = []
    for hh in range(acc.shape[1] // LANES):
        seg = acc[:, hh * LANES:(hh + 1) * LANES]
        partner = jnp.where(first_half, pltpu.roll(seg, LANES - 16, 1), pltpu.roll(seg, 16, 1))
        outs.append(seg * cos + partner * sin_signed)
    return jnp.concatenate(outs, axis=1)


def _proj_ctx_kernel(x_ref, mod_ref, g_ref, wk_ref, wv_ref, o_ref):
    h = _norm_mod(x_ref, mod_ref, g_ref)
    o_ref[0] = jnp.dot(h, wk_ref[...], preferred_element_type=F32).astype(o_ref.dtype)
    o_ref[1] = jnp.dot(h, wv_ref[...], preferred_element_type=F32).astype(o_ref.dtype)


def _proj_ctx(x2, mod3, g, w_bf, k_group):
    m, d = x2.shape
    tm = _tile(m, 512)
    return pl.pallas_call(
        _proj_ctx_kernel,
        grid=(m // tm,),
        in_specs=[pl.BlockSpec((tm, d), lambda i: (i, 0)),
                  pl.BlockSpec((1, N_MOD, d), lambda i: (0, 0, 0)),
                  pl.BlockSpec((1, d), lambda i: (0, 0)),
                  pl.BlockSpec((d, d), lambda i: (0, k_group)),
                  pl.BlockSpec((d, d), lambda i: (0, k_group + 1))],
        out_specs=pl.BlockSpec((2, tm, d), lambda i: (0, i, 0)),
        out_shape=jax.ShapeDtypeStruct((2, m, d), BF16),
        compiler_params=_cparams(("parallel",)),
        name="proj_ctx",
    )(x2, mod3, g, w_bf, w_bf)


def _proj_all_kernel(x_ref, mod_ref, g_ref, w_ref, cos_ref, sin_ref, u_ref, qkv_ref, gate_ref):
    d = x_ref.shape[1]
    h = _norm_mod(x_ref, mod_ref, g_ref)

    def group(n):
        return jnp.dot(h, w_ref[:, n * d:(n + 1) * d], preferred_element_type=F32)

    cos = cos_ref[...]
    sin = sin_ref[...]
    u_ref[...] = (group(0) * _sigmoid(group(1))).astype(u_ref.dtype)
    qkv_ref[0] = _rope(group(2) * (HEAD_DIM ** -0.5 * math.log2(math.e)), cos, sin).astype(qkv_ref.dtype)
    qkv_ref[1] = _rope(group(3), cos, sin).astype(qkv_ref.dtype)
    qkv_ref[2] = group(4).astype(qkv_ref.dtype)
    gate_ref[0] = _sigmoid(group(5)).astype(gate_ref.dtype)
    gate_ref[1] = _sigmoid(group(6)).astype(gate_ref.dtype)


def _proj_all(x2, mod3, g, w_bf, cos, sin_signed, t):
    m, d = x2.shape
    tm = _tile(t, 512)
    tblocks = t // tm
    return pl.pallas_call(
        _proj_all_kernel,
        grid=(m // tm,),
        in_specs=[pl.BlockSpec((tm, d), lambda i: (i, 0)),
                  pl.BlockSpec((1, N_MOD, d), lambda i: (i // tblocks, 0, 0)),
                  pl.BlockSpec((1, d), lambda i: (0, 0)),
                  pl.BlockSpec(w_bf.shape, lambda i: (0, 0), pipeline_mode=pl.Buffered(1)),
                  pl.BlockSpec((tm, LANES), lambda i: (i % tblocks, 0)),
                  pl.BlockSpec((tm, LANES), lambda i: (i % tblocks, 0))],
        out_specs=[pl.BlockSpec((tm, d), lambda i: (i, 0)),
                   pl.BlockSpec((3, tm, d), lambda i: (0, i, 0)),
                   pl.BlockSpec((2, tm, d), lambda i: (0, i, 0))],
        out_shape=[jax.ShapeDtypeStruct((m, d), BF16),
                   jax.ShapeDtypeStruct((3, m, d), BF16),
                   jax.ShapeDtypeStruct((2, m, d), BF16)],
        compiler_params=_cparams(("parallel",)),
        name="proj_all",
    )(x2, mod3, g, w_bf, cos, sin_signed)


def _conv_kernel(up_ref, um_ref, un_ref, gc_ref, wdw_ref, bdw_ref, lng_ref, lnb_ref, wc_ref, bc_ref,
                 o_ref, buf, cv, *, tt, rb):
    ti = pl.program_id(1)
    nt = pl.num_programs(1)
    d = um_ref.shape[2]
    prev = up_ref[0].astype(F32)
    nxt = un_ref[0].astype(F32)
    buf[0:HALO, :] = jnp.where(ti > 0, prev, 0.0)
    buf[HALO:HALO + tt, :] = um_ref[0].astype(F32)
    buf[HALO + tt:2 * HALO + tt, :] = jnp.where(ti < nt - 1, nxt, 0.0)

    off = HALO - CONV_HALF
    taps = [[(s - off, s // SUBLANES) for s in range(off, off + CONV_WIDTH) if s % SUBLANES == r]
            for r in range(SUBLANES)]
    n_slab = (off + CONV_WIDTH - 1) // SUBLANES + 1
    for c in range(d // LANES):
        cs = slice(c * LANES, (c + 1) * LANES)
        for rblk in range(tt // rb):
            r0 = rblk * rb
            slabs = [buf[r0 + SUBLANES * a:r0 + SUBLANES * a + rb + SUBLANES, cs] for a in range(n_slab)]
            acc = None
            for r in range(SUBLANES):
                z = None
                for k, a in taps[r]:
                    term = slabs[a] * wdw_ref[k:k + 1, cs]
                    z = term if z is None else z + term
                z = z[r:r + rb]
                acc = z if acc is None else acc + z
            cv[r0:r0 + rb, cs] = acc

    v = cv[...] + bdw_ref[...]
    mu = jnp.mean(v, axis=-1, keepdims=True)
    xc = v - mu
    var = jnp.mean(xc * xc, axis=-1, keepdims=True)
    y = xc * lax.rsqrt(var + EPS) * lng_ref[...] + lnb_ref[...]
    z = y * _sigmoid(y)
    yc = jnp.dot(z.astype(BF16), wc_ref[...], preferred_element_type=F32) + bc_ref[...]
    o_ref[...] = (gc_ref[0].astype(F32) * yc).astype(o_ref.dtype)


def _conv_branch(u3, gates, wdw, bdw, lng, lnb, wc_bf, bc):
    bsz, t, dc = u3.shape
    d = wc_bf.shape[1]
    tt = _tile(t, 256)
    nt = t // tt
    hb = tt // HALO
    nhb = t // HALO
    kern = functools.partial(_conv_kernel, tt=tt, rb=_tile(tt, 64))
    const = lambda shape: pl.BlockSpec(shape, lambda b, i: (0,) * len(shape))
    return pl.pallas_call(
        kern,
        grid=(bsz, nt),
        in_specs=[pl.BlockSpec((1, HALO, dc), lambda b, i: (b, jnp.maximum(i * hb - 1, 0), 0)),
                  pl.BlockSpec((1, tt, dc), lambda b, i: (b, i, 0)),
                  pl.BlockSpec((1, HALO, dc), lambda b, i: (b, jnp.minimum((i + 1) * hb, nhb - 1), 0)),
                  pl.BlockSpec((1, tt, d), lambda b, i: (0, b * nt + i, 0)),
                  const((CONV_WIDTH, dc)), const((1, dc)), const((1, dc)), const((1, dc)),
                  const((dc, d)), const((1, d))],
        out_specs=pl.BlockSpec((tt, d), lambda b, i: (b * nt + i, 0)),
        out_shape=jax.ShapeDtypeStruct((bsz * t, d), BF16),
        scratch_shapes=[pltpu.VMEM((tt + 2 * HALO, dc), F32), pltpu.VMEM((tt, dc), F32)],
        compiler_params=_cparams(("parallel", "parallel")),
        name="conv",
    )(u3, u3, u3, gates, wdw, bdw, lng, lnb, wc_bf, bc)


def _attn_kernel(lam_ref, q_ref, kl_ref, vl_ref, kc_ref, vc_ref, gs_ref, o_ref, vext, s_scr, m_scr,
                 *, tq, kb, nq, n_tiles, out_scale):
    n = pl.program_id(0)
    t = kl_ref.shape[1]
    pair = lax.div(n, nq)
    prev_pair = lax.div(jnp.maximum(n - 1, 0), nq)

    @pl.when((lax.rem(n, nq) == 0) & (n < n_tiles))
    def _():
        slot = lax.rem(pair, 2)
        vext[slot, 0:V_DIM, 0:t] = vl_ref[0].astype(F32).T.astype(vext.dtype)
        vext[slot, 0:V_DIM, t:] = vc_ref[0].astype(F32).T.astype(vext.dtype)
        row = lax.broadcasted_iota(jnp.int32, (vext.shape[1] - V_DIM, vext.shape[2]), 0)
        vext[slot, V_DIM:, :] = jnp.where(row == 0, 1.0, 0.0).astype(vext.dtype)

    @pl.when(n == 0)
    def _():
        s_scr[...] = jnp.zeros_like(s_scr)
        m_scr[...] = jnp.zeros_like(m_scr)

    q = q_ref[0]
    lane = lax.broadcasted_iota(jnp.int32, q.shape, 1)
    zero = jnp.zeros_like(q)
    qs = jnp.concatenate([jnp.where(lane < HEAD_DIM, q, zero), jnp.where(lane >= HEAD_DIM, q, zero)], axis=0)
    qs_t = qs.astype(F32).T.astype(BF16)
    vprev = vext.at[lax.rem(prev_pair, 2)]
    m_old = m_scr[...]
    acc = None
    m_new = None
    blocks = [(kl_ref, r, kb) for r in range(0, t, kb)] + [(kc_ref, 0, kc_ref.shape[1])]
    row = 0
    for k_ref, r0, size in blocks:
        rows = slice(row, row + size)
        p = jnp.exp2(s_scr[rows, :] - m_old).astype(BF16)
        part = jnp.dot(vprev[:, rows], p, preferred_element_type=F32)
        acc = part if acc is None else acc + part
        s_blk = jnp.dot(k_ref[0, r0:r0 + size, :], qs_t, preferred_element_type=F32)
        s_scr[rows, :] = s_blk
        blk_max = jnp.max(s_blk, axis=0, keepdims=True)
        m_new = blk_max if m_new is None else jnp.maximum(m_new, blk_max)
        row += size
    m_scr[...] = m_new

    o = acc[:V_DIM] / acc[V_DIM:V_DIM + 1]
    o = o[:, :tq] - lam_ref[0] * o[:, tq:]
    ms = jnp.mean(o * o, axis=0, keepdims=True)
    o = o * lax.rsqrt(ms + EPS) * gs_ref[...] * out_scale
    o_ref[...] = o.T.astype(o_ref.dtype)


def _attention(lam, qkv, kvc, g_subln, bsz, t, n_ctx, out_scale):
    m, width = qkv.shape[1], qkv.shape[2]
    nh = width // V_DIM
    tq = _tile(t, 512)
    nq = t // tq
    s_all = t + n_ctx
    n_tiles = bsz * nh * nq
    kern = functools.partial(_attn_kernel, tq=tq, kb=_tile(t, 256), nq=nq, n_tiles=n_tiles, out_scale=out_scale)

    def tile_coords(n):
        pair = lax.div(n, nq)
        return lax.div(pair, nh), lax.rem(pair, nh), lax.rem(n, nq)

    def started(n):
        return tile_coords(jnp.minimum(n, n_tiles - 1))

    def finished(n):
        return tile_coords(jnp.maximum(n - 1, 0))

    def q_map(n):
        b, h, j = started(n)
        return 0, b * nq + j, h

    def kv_map(which):
        def index(n):
            b, h, _ = started(n)
            return which, b, h
        return index

    def out_map(n):
        b, h, j = finished(n)
        return b * nq + j, h

    return pl.pallas_call(
        kern,
        grid=(n_tiles + 1,),
        in_specs=[pl.BlockSpec(memory_space=pltpu.SMEM),
                  pl.BlockSpec((1, tq, V_DIM), q_map),
                  pl.BlockSpec((1, t, V_DIM), kv_map(1)),
                  pl.BlockSpec((1, t, V_DIM), kv_map(2)),
                  pl.BlockSpec((1, n_ctx, V_DIM), kv_map(0)),
                  pl.BlockSpec((1, n_ctx, V_DIM), kv_map(1)),
                  pl.BlockSpec((V_DIM, 1), lambda n: (0, 0))],
        out_specs=pl.BlockSpec((tq, V_DIM), out_map),
        out_shape=jax.ShapeDtypeStruct((m, width), BF16),
        scratch_shapes=[pltpu.VMEM((2, V_DIM + 2 * SUBLANES, s_all), BF16),
                        pltpu.VMEM((s_all, 2 * tq), F32),
                        pltpu.VMEM((1, 2 * tq), F32)],
        compiler_params=_cparams(("arbitrary",)),
        name="attn",
    )(lam, qkv, qkv, qkv, kvc, kvc, g_subln.reshape(V_DIM, 1))


def _post_kernel(o_ref, ga_ref, gy_ref, x_ref, mod_ref, wa_ref, wo_ref, gf_ref, wrt_ref,
                 xmid_ref, hf_ref, aff_ref):
    y_attn = jnp.dot(o_ref[...], wa_ref[...], preferred_element_type=F32)
    merged = ga_ref[0].astype(F32) * y_attn + gy_ref[...].astype(F32)
    mix = jnp.dot(merged.astype(BF16), wo_ref[...], preferred_element_type=F32)
    xm = x_ref[...] + mod_ref[0, 2:3, :] * mix
    xmid_ref[...] = xm
    ms = jnp.mean(xm * xm, axis=-1, keepdims=True)
    hf = xm * lax.rsqrt(ms + EPS) * gf_ref[...]
    hf = hf * (1.0 + mod_ref[0, 4:5, :]) + mod_ref[0, 3:4, :]
    hf_ref[...] = hf.astype(hf_ref.dtype)
    logits = _dot3(wrt_ref[...], hf, _NT)
    z = jnp.exp(logits - jnp.max(logits, axis=0, keepdims=True))
    aff_ref[0] = z / jnp.sum(z, axis=0, keepdims=True)


def _post(o, gates, gy, x2, mod3, wa_bf, wo_bf, gf, wrt, bsz, t):
    m, d = x2.shape
    aw = o.shape[1]
    ne = wrt.shape[0]
    tm = _tile(t, 512)
    nt = t // tm
    const = lambda shape: pl.BlockSpec(shape, lambda i: (0,) * len(shape))
    return pl.pallas_call(
        _post_kernel,
        grid=(m // tm,),
        in_specs=[pl.BlockSpec((tm, aw), lambda i: (i, 0)),
                  pl.BlockSpec((1, tm, d), lambda i: (1, i, 0)),
                  pl.BlockSpec((tm, d), lambda i: (i, 0)),
                  pl.BlockSpec((tm, d), lambda i: (i, 0)),
                  pl.BlockSpec((1, N_MOD, d), lambda i: (i // nt, 0, 0)),
                  const((aw, d)), const((d, d)), const((1, d)), const((ne, d))],
        out_specs=[pl.BlockSpec((tm, d), lambda i: (i, 0)),
                   pl.BlockSpec((tm, d), lambda i: (i, 0)),
                   pl.BlockSpec((1, ne, tm), lambda i: (i // nt, 0, i % nt))],
        out_shape=[jax.ShapeDtypeStruct((m, d), F32),
                   jax.ShapeDtypeStruct((m, d), BF16),
                   jax.ShapeDtypeStruct((bsz, ne, t), F32)],
        compiler_params=_cparams(("parallel",)),
        name="post",
    )(o, gates, gy, x2, mod3, wa_bf, wo_bf, gf, wrt)


def _count(mask):
    ones = jnp.where(mask, 1.0, 0.0)
    return jnp.sum(jnp.sum(ones, axis=2, keepdims=True), axis=1, keepdims=True)


def _route_kernel(a_ref, pos_ref, *, cap):
    a = a_ref[0]
    ne, nc, ln = a.shape
    rows = ne * nc
    bits = lax.bitcast_convert_type(a, jnp.int32)

    def search(i, cur):
        cand = cur | jnp.left_shift(jnp.int32(1), 30 - i)
        return jnp.where(_count(bits >= cand) >= cap, cand, cur)

    tau = lax.fori_loop(0, 31, search, jnp.zeros((ne, 1, 1), jnp.int32))

    r_i = lax.broadcasted_iota(jnp.int32, (rows, rows), 0)
    r_j = lax.broadcasted_iota(jnp.int32, (rows, rows), 1)
    shift = int(math.log2(nc))
    same_expert = lax.shift_right_logical(r_i, shift) == lax.shift_right_logical(r_j, shift)
    chunk_before = jnp.where(same_expert & (r_j < r_i), 1.0, 0.0).astype(BF16)
    l_i = lax.broadcasted_iota(jnp.int32, (ln, ln), 0)
    l_j = lax.broadcasted_iota(jnp.int32, (ln, ln), 1)
    lane_upto = jnp.where(l_i <= l_j, 1.0, 0.0).astype(BF16)

    def excl_prefix(mask):
        xm = jnp.where(mask, 1.0, 0.0).reshape(rows, ln)
        incl = jnp.dot(xm.astype(BF16), lane_upto, preferred_element_type=F32)
        tot = jnp.broadcast_to(incl[:, ln - 1:ln], (rows, ln))
        offs = jnp.dot(chunk_before, tot.astype(BF16), preferred_element_type=F32)
        return (incl + offs - xm).reshape(ne, nc, ln)

    gt = bits > tau
    eq = bits == tau
    need = cap - _count(gt)
    sel = gt | (eq & (excl_prefix(eq) < need))
    pos = excl_prefix(sel)
    pos_ref[0] = jnp.where(sel, pos, -1.0).astype(jnp.int32)


def _route(aff4, cap):
    bsz, ne, nc, ln = aff4.shape
    assert nc & (nc - 1) == 0, "token chunks per sample must be a power of two"
    spec = pl.BlockSpec((1, ne, nc, ln), lambda b: (b, 0, 0, 0))
    return pl.pallas_call(
        functools.partial(_route_kernel, cap=cap),
        grid=(bsz,),
        in_specs=[spec],
        out_specs=spec,
        out_shape=jax.ShapeDtypeStruct((bsz, ne, nc, ln), jnp.int32),
        compiler_params=_cparams(("parallel",)),
        name="route",
    )(aff4)


def _gather_kernel(pos_ref, aff_ref, hf_ref, xs_ref, gate_ref, *, cap, tc):
    t = hf_ref.shape[1]
    d = hf_ref.shape[2]
    slot = lax.broadcasted_iota(jnp.int32, (cap, tc), 0)
    xs = jnp.zeros((cap, d), F32)
    gate = jnp.zeros((cap, 1), F32)
    for c in range(t // tc):
        cs = slice(c * tc, (c + 1) * tc)
        hit = pos_ref[0, 0, :, cs] == slot
        xs = xs + jnp.dot(jnp.where(hit, 1.0, 0.0).astype(BF16), hf_ref[0, cs, :], preferred_element_type=F32)
        gate = gate + jnp.sum(jnp.where(hit, aff_ref[0, 0, :, cs], 0.0), axis=1, keepdims=True)
    xs_ref[0, 0] = xs.astype(xs_ref.dtype)
    gate_ref[0, 0] = gate


def _gather(pos4, aff4, hf3, cap):
    bsz, ne, _, t = pos4.shape
    d = hf3.shape[2]
    kern = functools.partial(_gather_kernel, cap=cap, tc=_tile(t, 512))
    return pl.pallas_call(
        kern,
        grid=(bsz, ne),
        in_specs=[pl.BlockSpec((1, 1, 1, t), lambda b, e: (b, e, 0, 0)),
                  pl.BlockSpec((1, 1, 1, t), lambda b, e: (b, e, 0, 0)),
                  pl.BlockSpec((1, t, d), lambda b, e: (b, 0, 0))],
        out_specs=[pl.BlockSpec((1, 1, cap, d), lambda b, e: (e, b, 0, 0)),
                   pl.BlockSpec((1, 1, cap, 1), lambda b, e: (e, b, 0, 0))],
        out_shape=[jax.ShapeDtypeStruct((ne, bsz, cap, d), BF16),
                   jax.ShapeDtypeStruct((ne, bsz, cap, 1), F32)],
        compiler_params=_cparams(("parallel", "arbitrary")),
        name="gather",
    )(pos4, aff4, hf3)


def _expert_kernel(xs_ref, gate_ref, wg_ref, wu_ref, wd_ref, o_ref, acc):
    f = pl.program_id(2)

    @pl.when(f == 0)
    def _():
        acc[...] = jnp.zeros_like(acc)

    x = xs_ref[0]
    g = jnp.dot(x, wg_ref[0].astype(BF16), preferred_element_type=F32)
    u = jnp.dot(x, wu_ref[0].astype(BF16), preferred_element_type=F32)
    hid = (g * _sigmoid(g)) * u
    acc[...] += jnp.dot(hid.astype(BF16), wd_ref[0].astype(BF16), preferred_element_type=F32)

    @pl.when(f == pl.num_programs(2) - 1)
    def _():
        o_ref[0] = (acc[...] * gate_ref[0]).astype(o_ref.dtype)


def _experts(xs3, gate3, wg, wu, wd):
    ne, r, d = xs3.shape
    ff = wg.shape[2]
    tr = _tile(r, 2048)
    tf = _tile(ff, 256)
    return pl.pallas_call(
        _expert_kernel,
        grid=(ne, r // tr, ff // tf),
        in_specs=[pl.BlockSpec((1, tr, d), lambda e, i, f: (e, i, 0)),
                  pl.BlockSpec((1, tr, 1), lambda e, i, f: (e, i, 0)),
                  pl.BlockSpec((1, d, tf), lambda e, i, f: (e, 0, f)),
                  pl.BlockSpec((1, d, tf), lambda e, i, f: (e, 0, f)),
                  pl.BlockSpec((1, tf, d), lambda e, i, f: (e, f, 0))],
        out_specs=pl.BlockSpec((1, tr, d), lambda e, i, f: (e, i, 0)),
        out_shape=jax.ShapeDtypeStruct((ne, r, d), BF16),
        scratch_shapes=[pltpu.VMEM((tr, d), F32)],
        compiler_params=_cparams(("parallel", "parallel", "arbitrary")),
        name="experts",
    )(xs3, gate3, wg, wu, wd)


def _combine_kernel(pos_ref, ys_ref, xm_ref, mod_ref, gfin_ref, o_ref, *, cap):
    ne, tt = pos_ref.shape[1], pos_ref.shape[2]
    p = pos_ref[0].astype(F32)
    padded = jnp.concatenate([p, jnp.full((LANES - ne, tt), -1.0, F32)], axis=0)
    pos_t = padded.T
    slot = lax.broadcasted_iota(jnp.int32, (tt, cap), 1).astype(F32)
    acc = None
    for e in range(ne):
        onehot = jnp.where(pos_t[:, e:e + 1] == slot, 1.0, 0.0).astype(BF16)
        part = jnp.dot(onehot, ys_ref[e, 0], preferred_element_type=F32)
        acc = part if acc is None else acc + part
    xo = xm_ref[...] + mod_ref[0, 5:6, :] * acc
    ms = jnp.mean(xo * xo, axis=-1, keepdims=True)
    o_ref[...] = xo * lax.rsqrt(ms + EPS) * gfin_ref[...]


def _combine(pos3, ys4, xmid, mod3, gfin, cap):
    bsz, ne, t = pos3.shape
    m, d = xmid.shape
    tt = _tile(t, 512)
    nt = t // tt
    return pl.pallas_call(
        functools.partial(_combine_kernel, cap=cap),
        grid=(bsz, nt),
        in_specs=[pl.BlockSpec((1, ne, tt), lambda b, i: (b, 0, i)),
                  pl.BlockSpec((ne, 1, cap, d), lambda b, i: (0, b, 0, 0), pipeline_mode=pl.Buffered(1)),
                  pl.BlockSpec((tt, d), lambda b, i: (b * nt + i, 0)),
                  pl.BlockSpec((1, N_MOD, d), lambda b, i: (b, 0, 0)),
                  pl.BlockSpec((1, d), lambda b, i: (0, 0))],
        out_specs=pl.BlockSpec((tt, d), lambda b, i: (b * nt + i, 0)),
        out_shape=jax.ShapeDtypeStruct((m, d), F32),
        compiler_params=_cparams(("parallel", "arbitrary")),
        name="combine",
    )(pos3, ys4, xmid, mod3, gfin)


def _rope_tables(t):
    n_freq = HEAD_DIM // 4
    tok = jnp.arange(t)
    inv = ROPE_THETA ** (-jnp.arange(n_freq, dtype=F32) / n_freq)
    ang_r = (tok // GRID_W)[:, None].astype(F32) * inv
    ang_c = (tok % GRID_W)[:, None].astype(F32) * inv
    cos64 = jnp.concatenate([jnp.cos(ang_r)] * 2 + [jnp.cos(ang_c)] * 2, axis=-1)
    sin64 = jnp.concatenate([-jnp.sin(ang_r), jnp.sin(ang_r), -jnp.sin(ang_c), jnp.sin(ang_c)], axis=-1)
    return jnp.tile(cos64, (1, 2)), jnp.tile(sin64, (1, 2))


def kernel(x, c, ctx, c_ctx, w_ada, b_ada, g_norm_mix, g_norm_ffn, w_in, w_dw, b_dw, ln_g_conv, ln_b_conv,
           w_conv_out, b_conv_out, lambda_q1, lambda_k1, lambda_q2, lambda_k2, g_subln, w_attn_out, w_out,
           w_router, w_expert_gate, w_expert_up, w_expert_down, g_final):
    bsz, t, d = x.shape
    n_ctx = ctx.shape[1]
    depth = w_ada.shape[0]
    assert depth == 1, "single-layer trunk: the context stream only feeds keys/values"
    ne = w_router.shape[2]
    cap = CAPACITY_FACTOR * t // ne
    dc = w_dw.shape[2]
    qkw = N_HEADS * 2 * HEAD_DIM
    aw = N_HEADS * V_DIM
    assert dc == d and qkw == d and aw == d, "projection groups are addressed as equal-width column blocks"
    layer = 0
    lam_init = 0.8 - 0.6 * math.exp(-0.3 * layer)
    lam = (jnp.exp(jnp.sum(lambda_q1[layer].astype(F32) * lambda_k1[layer].astype(F32)))
           - jnp.exp(jnp.sum(lambda_q2[layer].astype(F32) * lambda_k2[layer].astype(F32)))
           + lam_init).reshape(1)

    row = lambda v: v.reshape(1, -1)
    x2 = x.reshape(bsz * t, d)
    ctx2 = ctx.reshape(bsz * n_ctx, d)

    pad = (-(bsz + 1)) % 8
    cc = jnp.concatenate([c, c_ctx[None, :], jnp.zeros((pad, d), F32)], axis=0)
    mod = _adaln(cc, w_ada[layer], row(b_ada[layer]))
    mod_lat = mod[:bsz].reshape(bsz, N_MOD, d)
    mod_ctx = mod[bsz:bsz + 1].reshape(1, N_MOD, d)

    w_in_bf = w_in[layer].astype(BF16)
    g_mix = row(g_norm_mix[layer])
    cos, sin_signed = _rope_tables(t)

    u, qkv, gates = _proj_all(x2, mod_lat, g_mix, w_in_bf, cos, sin_signed, t)
    kvc = _proj_ctx(ctx2, mod_ctx, g_mix, w_in_bf, 3)

    gy = _conv_branch(u.reshape(bsz, t, dc), gates, w_dw[layer], row(b_dw[layer]), row(ln_g_conv[layer]),
                      row(ln_b_conv[layer]), w_conv_out[layer].astype(BF16), row(b_conv_out[layer]))
    o = _attention(lam, qkv, kvc, row(g_subln[layer]), bsz, t, n_ctx, 1.0 - lam_init)

    xmid, hf, aff_t = _post(o, gates, gy, x2, mod_lat, w_attn_out[layer].astype(BF16), w_out[layer].astype(BF16),
                            row(g_norm_ffn[layer]), w_router[layer].T, bsz, t)

    pos = _route(aff_t.reshape(bsz, ne, t // LANES, LANES), cap)
    xs, gate = _gather(pos.reshape(bsz, ne, 1, t), aff_t.reshape(bsz, ne, 1, t), hf.reshape(bsz, t, d), cap)
    ys = _experts(xs.reshape(ne, bsz * cap, d), gate.reshape(ne, bsz * cap, 1),
                  w_expert_gate[layer], w_expert_up[layer], w_expert_down[layer])
    out = _combine(pos.reshape(bsz, ne, t), ys.reshape(ne, bsz, cap, d), xmid, mod_lat, row(g_final), cap)
    return out.reshape(bsz, t, d)
```

```python
import functools
import math

import jax
import jax.numpy as jnp
from jax import lax
from jax.experimental import pallas as pl
from jax.experimental.pallas import tpu as pltpu

N_HEADS = 8
HEAD_DIM = 64
V_DIM = 2 * HEAD_DIM
CONV_WIDTH = 31
CONV_HALF = CONV_WIDTH // 2
N_MOD = 6
GRID_W = 64
ROPE_THETA = 10000.0
CAPACITY_FACTOR = 2
EPS = 1e-6

LANES = 128
SUBLANES = 8
HALO = 16
VMEM_LIMIT = 56 * 1024 * 1024

F32 = jnp.float32
BF16 = jnp.bfloat16


def _cparams(sem):
    return pltpu.CompilerParams(dimension_semantics=sem, vmem_limit_bytes=VMEM_LIMIT)


def _tile(n, pref):
    t = min(n, pref)
    while n % t:
        t //= 2
    return t


def _split_bf16(a):
    hi = a.astype(BF16)
    lo = (a - hi.astype(F32)).astype(BF16)
    return hi, lo


def _dot3(a, b, dims):
    ah, al = _split_bf16(a)
    bh, bl = _split_bf16(b)
    dg = functools.partial(lax.dot_general, dimension_numbers=dims, preferred_element_type=F32)
    return dg(ah, bh) + (dg(ah, bl) + dg(al, bh))


_NN = (((1,), (0,)), ((), ()))
_NT = (((1,), (1,)), ((), ()))


def _sigmoid(x):
    return 1.0 / (1.0 + jnp.exp(-x))


def _adaln_kernel(c_ref, w_ref, b_ref, o_ref):
    c = c_ref[...]
    s = c * _sigmoid(c)
    o_ref[...] = _dot3(s, w_ref[...], _NN) + b_ref[...]


def _adaln(cc, w, b):
    rows, d = cc.shape
    n = w.shape[1]
    tn = _tile(n, 1024)
    return pl.pallas_call(
        _adaln_kernel,
        grid=(n // tn,),
        in_specs=[pl.BlockSpec((rows, d), lambda j: (0, 0)),
                  pl.BlockSpec((d, tn), lambda j: (0, j)),
                  pl.BlockSpec((1, tn), lambda j: (0, j))],
        out_specs=pl.BlockSpec((rows, tn), lambda j: (0, j)),
        out_shape=jax.ShapeDtypeStruct((rows, n), F32),
        compiler_params=_cparams(("arbitrary",)),
        name="adaln",
    )(cc, w, b)


def _norm_mod(x_ref, mod_ref, g_ref):
    x = x_ref[...]
    ms = jnp.mean(x * x, axis=-1, keepdims=True)
    y = x * lax.rsqrt(ms + EPS) * g_ref[...]
    return (y * (1.0 + mod_ref[0, 1:2, :]) + mod_ref[0, 0:1, :]).astype(BF16)


def _rope(acc, cos, sin_signed):
    lane = lax.broadcasted_iota(jnp.int32, (acc.shape[0], LANES), 1)
    first_half = (lane % 32) < 16
    outs = []
    for hh in range(acc.shape[1] // LANES):
        seg = acc[:, hh * LANES:(hh + 1) * LANES]
        partner = jnp.where(first_half, pltpu.roll(seg, LANES - 16, 1), pltpu.roll(seg, 16, 1))
        outs.append(seg * cos + partner * sin_signed)
    return jnp.concatenate(outs, axis=1)


def _proj_ctx_kernel(x_ref, mod_ref, g_ref, wk_ref, wv_ref, o_ref):
    h = _norm_mod(x_ref, mod_ref, g_ref)
    o_ref[0] = jnp.dot(h, wk_ref[...], preferred_element_type=F32).astype(o_ref.dtype)
    o_ref[1] = jnp.dot(h, wv_ref[...], preferred_element_type=F32).astype(o_ref.dtype)


def _proj_ctx(x2, mod3, g, w_bf, k_group):
    m, d = x2.shape
    tm = _tile(m, 512)
    return pl.pallas_call(
        _proj_ctx_kernel,
        grid=(m // tm,),
        in_specs=[pl.BlockSpec((tm, d), lambda i: (i, 0)),
                  pl.BlockSpec((1, N_MOD, d), lambda i: (0, 0, 0)),
                  pl.BlockSpec((1, d), lambda i: (0, 0)),
                  pl.BlockSpec((d, d), lambda i: (0, k_group)),
                  pl.BlockSpec((d, d), lambda i: (0, k_group + 1))],
        out_specs=pl.BlockSpec((2, tm, d), lambda i: (0, i, 0)),
        out_shape=jax.ShapeDtypeStruct((2, m, d), BF16),
        compiler_params=_cparams(("parallel",)),
        name="proj_ctx",
    )(x2, mod3, g, w_bf, w_bf)


def _proj_all_kernel(x_ref, mod_ref, g_ref, w_ref, cos_ref, sin_ref, u_ref, qkv_ref, gate_ref):
    d = x_ref.shape[1]
    h = _norm_mod(x_ref, mod_ref, g_ref)

    def group(n):
        return jnp.dot(h, w_ref[:, n * d:(n + 1) * d], preferred_element_type=F32)

    cos = cos_ref[...]
    sin = sin_ref[...]
    u_ref[...] = (group(0) * _sigmoid(group(1))).astype(u_ref.dtype)
    qkv_ref[0] = _rope(group(2) * (HEAD_DIM ** -0.5 * math.log2(math.e)), cos, sin).astype(qkv_ref.dtype)
    qkv_ref[1] = _rope(group(3), cos, sin).astype(qkv_ref.dtype)
    qkv_ref[2] = group(4).astype(qkv_ref.dtype)
    gate_ref[0] = _sigmoid(group(5)).astype(gate_ref.dtype)
    gate_ref[1] = _sigmoid(group(6)).astype(gate_ref.dtype)


def _proj_all(x2, mod3, g, w_bf, cos, sin_signed, t):
    m, d = x2.shape
    tm = _tile(t, 512)
    tblocks = t // tm
    return pl.pallas_call(
        _proj_all_kernel,
        grid=(m // tm,),
        in_specs=[pl.BlockSpec((tm, d), lambda i: (i, 0)),
                  pl.BlockSpec((1, N_MOD, d), lambda i: (i // tblocks, 0, 0)),
                  pl.BlockSpec((1, d), lambda i: (0, 0)),
                  pl.BlockSpec(w_bf.shape, lambda i: (0, 0), pipeline_mode=pl.Buffered(1)),
                  pl.BlockSpec((tm, LANES), lambda i: (i % tblocks, 0)),
                  pl.BlockSpec((tm, LANES), lambda i: (i % tblocks, 0))],
        out_specs=[pl.BlockSpec((tm, d), lambda i: (i, 0)),
                   pl.BlockSpec((3, tm, d), lambda i: (0, i, 0)),
                   pl.BlockSpec((2, tm, d), lambda i: (0, i, 0))],
        out_shape=[jax.ShapeDtypeStruct((m, d), BF16),
                   jax.ShapeDtypeStruct((3, m, d), BF16),
                   jax.ShapeDtypeStruct((2, m, d), BF16)],
        compiler_params=_cparams(("parallel",)),
        name="proj_all",
    )(x2, mod3, g, w_bf, cos, sin_signed)


def _conv_kernel(up_ref, um_ref, un_ref, gc_ref, wdw_ref, bdw_ref, lng_ref, lnb_ref, wc_ref, bc_ref,
                 o_ref, buf, cv, *, tt, rb):
    ti = pl.program_id(1)
    nt = pl.num_programs(1)
    d = um_ref.shape[2]
    prev = up_ref[0].astype(F32)
    nxt = un_ref[0].astype(F32)
    buf[0:HALO, :] = jnp.where(ti > 0, prev, 0.0)
    buf[HALO:HALO + tt, :] = um_ref[0].astype(F32)
    buf[HALO + tt:2 * HALO + tt, :] = jnp.where(ti < nt - 1, nxt, 0.0)

    off = HALO - CONV_HALF
    taps = [[(s - off, s // SUBLANES) for s in range(off, off + CONV_WIDTH) if s % SUBLANES == r]
            for r in range(SUBLANES)]
    n_slab = (off + CONV_WIDTH - 1) // SUBLANES + 1
    for c in range(d // LANES):
        cs = slice(c * LANES, (c + 1) * LANES)
        for rblk in range(tt // rb):
            r0 = rblk * rb
            slabs = [buf[r0 + SUBLANES * a:r0 + SUBLANES * a + rb + SUBLANES, cs] for a in range(n_slab)]
            acc = None
            for r in range(SUBLANES):
                z = None
                for k, a in taps[r]:
                    term = slabs[a] * wdw_ref[k:k + 1, cs]
                    z = term if z is None else z + term
                z = z[r:r + rb]
                acc = z if acc is None else acc + z
            cv[r0:r0 + rb, cs] = acc

    v = cv[...] + bdw_ref[...]
    mu = jnp.mean(v, axis=-1, keepdims=True)
    xc = v - mu
    var = jnp.mean(xc * xc, axis=-1, keepdims=True)
    y = xc * lax.rsqrt(var + EPS) * lng_ref[...] + lnb_ref[...]
    z = y * _sigmoid(y)
    yc = jnp.dot(z.astype(BF16), wc_ref[...], preferred_element_type=F32) + bc_ref[...]
    o_ref[...] = (gc_ref[0].astype(F32) * yc).astype(o_ref.dtype)


def _conv_branch(u3, gates, wdw, bdw, lng, lnb, wc_bf, bc):
    bsz, t, dc = u3.shape
    d = wc_bf.shape[1]
    tt = _tile(t, 256)
    nt = t // tt
    hb = tt // HALO
    nhb = t // HALO
    kern = functools.partial(_conv_kernel, tt=tt, rb=_tile(tt, 64))
    const = lambda shape: pl.BlockSpec(shape, lambda b, i: (0,) * len(shape))
    return pl.pallas_call(
        kern,
        grid=(bsz, nt),
        in_specs=[pl.BlockSpec((1, HALO, dc), lambda b, i: (b, jnp.maximum(i * hb - 1, 0), 0)),
                  pl.BlockSpec((1, tt, dc), lambda b, i: (b, i, 0)),
                  pl.BlockSpec((1, HALO, dc), lambda b, i: (b, jnp.minimum((i + 1) * hb, nhb - 1), 0)),
                  pl.BlockSpec((1, tt, d), lambda b, i: (0, b * nt + i, 0)),
                  const((CONV_WIDTH, dc)), const((1, dc)), const((1, dc)), const((1, dc)),
                  const((dc, d)), const((1, d))],
        out_specs=pl.BlockSpec((tt, d), lambda b, i: (b * nt + i, 0)),
        out_shape=jax.ShapeDtypeStruct((bsz * t, d), BF16),
        scratch_shapes=[pltpu.VMEM((tt + 2 * HALO, dc), F32), pltpu.VMEM((tt, dc), F32)],
        compiler_params=_cparams(("parallel", "parallel")),
        name="conv",
    )(u3, u3, u3, gates, wdw, bdw, lng, lnb, wc_bf, bc)


def _attn_kernel(lam_ref, q_ref, kl_ref, vl_ref, kc_ref, vc_ref, gs_ref, o_ref, vext, s_scr, m_scr,
                 *, tq, kb, nq, n_tiles, out_scale):
    n = pl.program_id(0)
    t = kl_ref.shape[1]
    pair = lax.div(n, nq)
    prev_pair = lax.div(jnp.maximum(n - 1, 0), nq)

    @pl.when((lax.rem(n, nq) == 0) & (n < n_tiles))
    def _():
        slot = lax.rem(pair, 2)
        vext[slot, 0:V_DIM, 0:t] = vl_ref[0].astype(F32).T.astype(vext.dtype)
        vext[slot, 0:V_DIM, t:] = vc_ref[0].astype(F32).T.astype(vext.dtype)
        row = lax.broadcasted_iota(jnp.int32, (vext.shape[1] - V_DIM, vext.shape[2]), 0)
        vext[slot, V_DIM:, :] = jnp.where(row == 0, 1.0, 0.0).astype(vext.dtype)

    @pl.when(n == 0)
    def _():
        s_scr[...] = jnp.zeros_like(s_scr)
        m_scr[...] = jnp.zeros_like(m_scr)

    q = q_ref[0]
    lane = lax.broadcasted_iota(jnp.int32, q.shape, 1)
    zero = jnp.zeros_like(q)
    qs = jnp.concatenate([jnp.where(lane < HEAD_DIM, q, zero), jnp.where(lane >= HEAD_DIM, q, zero)], axis=0)
    qs_t = qs.astype(F32).T.astype(BF16)
    vprev = vext.at[lax.rem(prev_pair, 2)]
    m_old = m_scr[...]
    acc = None
    m_new = None
    blocks = [(kl_ref, r, kb) for r in range(0, t, kb)] + [(kc_ref, 0, kc_ref.shape[1])]
    row = 0
    for k_ref, r0, size in blocks:
        rows = slice(row, row + size)
        p = jnp.exp2(s_scr[rows, :] - m_old).astype(BF16)
        part = jnp.dot(vprev[:, rows], p, preferred_element_type=F32)
        acc = part if acc is None else acc + part
        s_blk = jnp.dot(k_ref[0, r0:r0 + size, :], qs_t, preferred_element_type=F32)
        s_scr[rows, :] = s_blk
        blk_max = jnp.max(s_blk, axis=0, keepdims=True)
        m_new = blk_max if m_new is None else jnp.maximum(m_new, blk_max)
        row += size
    m_scr[...] = m_new

    o = acc[:V_DIM] / acc[V_DIM:V_DIM + 1]
    o = o[:, :tq] - lam_ref[0] * o[:, tq:]
    ms = jnp.mean(o * o, axis=0, keepdims=True)
    o = o * lax.rsqrt(ms + EPS) * gs_ref[...] * out_scale
    o_ref[...] = o.T.astype(o_ref.dtype)


def _attention(lam, qkv, kvc, g_subln, bsz, t, n_ctx, out_scale):
    m, width = qkv.shape[1], qkv.shape[2]
    nh = width // V_DIM
    tq = _tile(t, 512)
    nq = t // tq
    s_all = t + n_ctx
    n_tiles = bsz * nh * nq
    kern = functools.partial(_attn_kernel, tq=tq, kb=_tile(t, 256), nq=nq, n_tiles=n_tiles, out_scale=out_scale)

    def tile_coords(n):
        pair = lax.div(n, nq)
        return lax.div(pair, nh), lax.rem(pair, nh), lax.rem(n, nq)

    def started(n):
        return tile_coords(jnp.minimum(n, n_tiles - 1))

    def finished(n):
        return tile_coords(jnp.maximum(n - 1, 0))

    def q_map(n):
        b, h, j = started(n)
        return 0, b * nq + j, h

    def kv_map(which):
        def index(n):
            b, h, _ = started(n)
            return which, b, h
        return index

    def out_map(n):
        b, h, j = finished(n)
        return b * nq + j, h

    return pl.pallas_call(
        kern,
        grid=(n_tiles + 1,),
        in_specs=[pl.BlockSpec(memory_space=pltpu.SMEM),
                  pl.BlockSpec((1, tq, V_DIM), q_map),
                  pl.BlockSpec((1, t, V_DIM), kv_map(1)),
                  pl.BlockSpec((1, t, V_DIM), kv_map(2)),
                  pl.BlockSpec((1, n_ctx, V_DIM), kv_map(0)),
                  pl.BlockSpec((1, n_ctx, V_DIM), kv_map(1)),
                  pl.BlockSpec((V_DIM, 1), lambda n: (0, 0))],
        out_specs=pl.BlockSpec((tq, V_DIM), out_map),
        out_shape=jax.ShapeDtypeStruct((m, width), BF16),
        scratch_shapes=[pltpu.VMEM((2, V_DIM + 2 * SUBLANES, s_all), BF16),
                        pltpu.VMEM((s_all, 2 * tq), F32),
                        pltpu.VMEM((1, 2 * tq), F32)],
        compiler_params=_cparams(("arbitrary",)),
        name="attn",
    )(lam, qkv, qkv, qkv, kvc, kvc, g_subln.reshape(V_DIM, 1))


def _post_kernel(o_ref, ga_ref, gy_ref, x_ref, mod_ref, wa_ref, wo_ref, gf_ref, wrt_ref,
                 xmid_ref, hf_ref, aff_ref):
    y_attn = jnp.dot(o_ref[...], wa_ref[...], preferred_element_type=F32)
    merged = ga_ref[0].astype(F32) * y_attn + gy_ref[...].astype(F32)
    mix = jnp.dot(merged.astype(BF16), wo_ref[...], preferred_element_type=F32)
    xm = x_ref[...] + mod_ref[0, 2:3, :] * mix
    xmid_ref[...] = xm
    ms = jnp.mean(xm * xm, axis=-1, keepdims=True)
    hf = xm * lax.rsqrt(ms + EPS) * gf_ref[...]
    hf = hf * (1.0 + mod_ref[0, 4:5, :]) + mod_ref[0, 3:4, :]
    hf_ref[...] = hf.astype(hf_ref.dtype)
    logits = _dot3(wrt_ref[...], hf, _NT)
    z = jnp.exp(logits - jnp.max(logits, axis=0, keepdims=True))
    aff_ref[0] = z / jnp.sum(z, axis=0, keepdims=True)


def _post(o, gates, gy, x2, mod3, wa_bf, wo_bf, gf, wrt, bsz, t):
    m, d = x2.shape
    aw = o.shape[1]
    ne = wrt.shape[0]
    tm = _tile(t, 512)
    nt = t // tm
    const = lambda shape: pl.BlockSpec(shape, lambda i: (0,) * len(shape))
    return pl.pallas_call(
        _post_kernel,
        grid=(m // tm,),
        in_specs=[pl.BlockSpec((tm, aw), lambda i: (i, 0)),
                  pl.BlockSpec((1, tm, d), lambda i: (1, i, 0)),
                  pl.BlockSpec((tm, d), lambda i: (i, 0)),
                  pl.BlockSpec((tm, d), lambda i: (i, 0)),
                  pl.BlockSpec((1, N_MOD, d), lambda i: (i // nt, 0, 0)),
                  const((aw, d)), const((d, d)), const((1, d)), const((ne, d))],
        out_specs=[pl.BlockSpec((tm, d), lambda i: (i, 0)),
                   pl.BlockSpec((tm, d), lambda i: (i, 0)),
                   pl.BlockSpec((1, ne, tm), lambda i: (i // nt, 0, i % nt))],
        out_shape=[jax.ShapeDtypeStruct((m, d), F32),
                   jax.ShapeDtypeStruct((m, d), BF16),
                   jax.ShapeDtypeStruct((bsz, ne, t), F32)],
        compiler_params=_cparams(("parallel",)),
        name="post",
    )(o, gates, gy, x2, mod3, wa_bf, wo_bf, gf, wrt)


def _count(mask):
    ones = jnp.where(mask, 1.0, 0.0)
    return jnp.sum(jnp.sum(ones, axis=2, keepdims=True), axis=1, keepdims=True)


def _route_kernel(a_ref, pos_ref, rank_ref, *, cap):
    a = a_ref[0]
    ne, nc, ln = a.shape
    rows = ne * nc
    bits = lax.bitcast_convert_type(a, jnp.int32)

    def search(i, cur):
        cand = cur | jnp.left_shift(jnp.int32(1), 30 - i)
        return jnp.where(_count(bits >= cand) >= cap, cand, cur)

    tau = lax.fori_loop(0, 31, search, jnp.zeros((ne, 1, 1), jnp.int32))

    r_i = lax.broadcasted_iota(jnp.int32, (rows, rows), 0)
    r_j = lax.broadcasted_iota(jnp.int32, (rows, rows), 1)
    shift = int(math.log2(nc))
    same_expert = lax.shift_right_logical(r_i, shift) == lax.shift_right_logical(r_j, shift)
    chunk_before = jnp.where(same_expert & (r_j < r_i), 1.0, 0.0).astype(BF16)
    l_i = lax.broadcasted_iota(jnp.int32, (ln, ln), 0)
    l_j = lax.broadcasted_iota(jnp.int32, (ln, ln), 1)
    lane_upto = jnp.where(l_i <= l_j, 1.0, 0.0).astype(BF16)

    def excl_prefix(mask):
        xm = jnp.where(mask, 1.0, 0.0).reshape(rows, ln)
        incl = jnp.dot(xm.astype(BF16), lane_upto, preferred_element_type=F32)
        tot = jnp.broadcast_to(incl[:, ln - 1:ln], (rows, ln))
        offs = jnp.dot(chunk_before, tot.astype(BF16), preferred_element_type=F32)
        return (incl + offs - xm).reshape(ne, nc, ln)

    gt = bits > tau
    eq = bits == tau
    need = cap - _count(gt)
    sel = gt | (eq & (excl_prefix(eq) < need))
    pos = excl_prefix(sel)
    pos_ref[0] = jnp.where(sel, pos, -1.0).astype(jnp.int32)
    rank_ref[0] = pos.astype(jnp.int32)


def _route(aff4, cap):
    bsz, ne, nc, ln = aff4.shape
    assert nc & (nc - 1) == 0, "token chunks per sample must be a power of two"
    spec = pl.BlockSpec((1, ne, nc, ln), lambda b: (b, 0, 0, 0))
    return pl.pallas_call(
        functools.partial(_route_kernel, cap=cap),
        grid=(bsz,),
        in_specs=[spec],
        out_specs=[spec, spec],
        out_shape=[jax.ShapeDtypeStruct((bsz, ne, nc, ln), jnp.int32)] * 2,
        compiler_params=_cparams(("parallel",)),
        name="route",
    )(aff4)


def _gather_kernel(cs_ref, pos_ref, aff_ref, hf_ref, xs_ref, gate_ref, acc, gacc, *, cap, tc, win):
    b, e = pl.program_id(0), pl.program_id(1)
    nchunk = hf_ref.shape[1] // tc
    base = (b * pl.num_programs(1) + e) * (nchunk + 1)
    acc[...] = jnp.zeros_like(acc)
    gacc[...] = jnp.zeros_like(gacc)
    for c in range(nchunk):
        lo, hi = cs_ref[base + c], cs_ref[base + c + 1]
        r0 = lax.shift_right_logical(lo, 4) * 16
        n_win = lax.div(hi - r0 + (win - 1), win)
        cols = slice(c * tc, (c + 1) * tc)

        def window(w, carry):
            r = pl.multiple_of(r0 + w * win, 16)
            slot = r + lax.broadcasted_iota(jnp.int32, (win, tc), 0)
            hit = pos_ref[0, 0, :, cols] == slot
            acc[pl.ds(r, win), :] += jnp.dot(jnp.where(hit, 1.0, 0.0).astype(BF16), hf_ref[0, cols, :],
                                             preferred_element_type=F32)
            gacc[pl.ds(r, win), :] += jnp.sum(jnp.where(hit, aff_ref[0, 0, :, cols], 0.0), axis=1, keepdims=True)
            return carry

        lax.fori_loop(0, n_win, window, 0)
    xs_ref[0, 0] = acc[0:cap, :].astype(xs_ref.dtype)
    gate_ref[0, 0] = gacc[0:cap, :]


def _gather(chunk_slots, pos4, aff4, hf3, cap, tc):
    bsz, ne, _, t = pos4.shape
    d = hf3.shape[2]
    win = _tile(cap, 128)
    kern = functools.partial(_gather_kernel, cap=cap, tc=tc, win=win)
    return pl.pallas_call(
        kern,
        grid_spec=pltpu.PrefetchScalarGridSpec(
            num_scalar_prefetch=1,
            grid=(bsz, ne),
            in_specs=[pl.BlockSpec((1, 1, 1, t), lambda b, e, cs: (b, e, 0, 0)),
                      pl.BlockSpec((1, 1, 1, t), lambda b, e, cs: (b, e, 0, 0)),
                      pl.BlockSpec((1, t, d), lambda b, e, cs: (b, 0, 0))],
            out_specs=[pl.BlockSpec((1, 1, cap, d), lambda b, e, cs: (e, b, 0, 0)),
                       pl.BlockSpec((1, 1, cap, 1), lambda b, e, cs: (e, b, 0, 0))],
            scratch_shapes=[pltpu.VMEM((cap + win, d), F32), pltpu.VMEM((cap + win, 1), F32)]),
        out_shape=[jax.ShapeDtypeStruct((ne, bsz, cap, d), BF16),
                   jax.ShapeDtypeStruct((ne, bsz, cap, 1), F32)],
        compiler_params=_cparams(("parallel", "arbitrary")),
        name="gather",
    )(chunk_slots, pos4, aff4, hf3)


def _expert_kernel(xs_ref, gate_ref, wg_ref, wu_ref, wd_ref, o_ref, acc):
    f = pl.program_id(2)

    @pl.when(f == 0)
    def _():
        acc[...] = jnp.zeros_like(acc)

    x = xs_ref[0]
    g = jnp.dot(x, wg_ref[0].astype(BF16), preferred_element_type=F32)
    u = jnp.dot(x, wu_ref[0].astype(BF16), preferred_element_type=F32)
    hid = (g * _sigmoid(g)) * u
    acc[...] += jnp.dot(hid.astype(BF16), wd_ref[0].astype(BF16), preferred_element_type=F32)

    @pl.when(f == pl.num_programs(2) - 1)
    def _():
        o_ref[0] = (acc[...] * gate_ref[0]).astype(o_ref.dtype)


def _experts(xs3, gate3, wg, wu, wd):
    ne, r, d = xs3.shape
    ff = wg.shape[2]
    tr = _tile(r, 2048)
    tf = _tile(ff, 256)
    return pl.pallas_call(
        _expert_kernel,
        grid=(ne, r // tr, ff // tf),
        in_specs=[pl.BlockSpec((1, tr, d), lambda e, i, f: (e, i, 0)),
                  pl.BlockSpec((1, tr, 1), lambda e, i, f: (e, i, 0)),
                  pl.BlockSpec((1, d, tf), lambda e, i, f: (e, 0, f)),
                  pl.BlockSpec((1, d, tf), lambda e, i, f: (e, 0, f)),
                  pl.BlockSpec((1, tf, d), lambda e, i, f: (e, f, 0))],
        out_specs=pl.BlockSpec((1, tr, d), lambda e, i, f: (e, i, 0)),
        out_shape=jax.ShapeDtypeStruct((ne, r, d), BF16),
        scratch_shapes=[pltpu.VMEM((tr, d), F32)],
        compiler_params=_cparams(("parallel", "parallel", "arbitrary")),
        name="experts",
    )(xs3, gate3, wg, wu, wd)


def _combine_kernel(cs_ref, pos_ref, ys_ref, xm_ref, mod_ref, gfin_ref, o_ref, acc, pos_t, *, cap, win):
    b, i = pl.program_id(0), pl.program_id(1)
    nchunk = pl.num_programs(1)
    ne, tt = pos_ref.shape[1], pos_ref.shape[2]
    p = pos_ref[0].astype(F32)
    padded = jnp.concatenate([p, jnp.full((LANES - ne, tt), -1.0, F32)], axis=0)
    pos_t[...] = padded.T
    acc[...] = jnp.zeros_like(acc)
    for e in range(ne):
        base = (b * ne + e) * (nchunk + 1) + i
        lo, hi = cs_ref[base], cs_ref[base + 1]
        k0 = lax.shift_right_logical(lo, 4) * 16
        n_win = lax.div(hi - k0 + (win - 1), win)

        def window(w, carry):
            first = k0 + w * win
            k = pl.multiple_of(jnp.minimum(first, cap - win), 16)
            slot = k + lax.broadcasted_iota(jnp.int32, (tt, win), 1)
            mine = (pos_t[:, e:e + 1] == slot.astype(F32)) & (slot >= first)
            acc[...] += jnp.dot(jnp.where(mine, 1.0, 0.0).astype(BF16), ys_ref[e, 0, pl.ds(k, win), :],
                                preferred_element_type=F32)
            return carry

        lax.fori_loop(0, n_win, window, 0)
    xo = xm_ref[...] + mod_ref[0, 5:6, :] * acc[...]
    ms = jnp.mean(xo * xo, axis=-1, keepdims=True)
    o_ref[...] = xo * lax.rsqrt(ms + EPS) * gfin_ref[...]


def _combine(chunk_slots, pos3, ys4, xmid, mod3, gfin, cap, tc):
    bsz, ne, t = pos3.shape
    m, d = xmid.shape
    nt = t // tc
    return pl.pallas_call(
        functools.partial(_combine_kernel, cap=cap, win=_tile(cap, 128)),
        grid_spec=pltpu.PrefetchScalarGridSpec(
            num_scalar_prefetch=1,
            grid=(bsz, nt),
            in_specs=[pl.BlockSpec((1, ne, tc), lambda b, i, cs: (b, 0, i)),
                      pl.BlockSpec((ne, 1, cap, d), lambda b, i, cs: (0, b, 0, 0), pipeline_mode=pl.Buffered(1)),
                      pl.BlockSpec((tc, d), lambda b, i, cs: (b * nt + i, 0)),
                      pl.BlockSpec((1, N_MOD, d), lambda b, i, cs: (b, 0, 0)),
                      pl.BlockSpec((1, d), lambda b, i, cs: (0, 0))],
            out_specs=pl.BlockSpec((tc, d), lambda b, i, cs: (b * nt + i, 0)),
            scratch_shapes=[pltpu.VMEM((tc, d), F32), pltpu.VMEM((tc, LANES), F32)]),
        out_shape=jax.ShapeDtypeStruct((m, d), F32),
        compiler_params=_cparams(("parallel", "arbitrary")),
        name="combine",
    )(chunk_slots, pos3, ys4, xmid, mod3, gfin)


def _rope_tables(t):
    n_freq = HEAD_DIM // 4
    tok = jnp.arange(t)
    inv = ROPE_THETA ** (-jnp.arange(n_freq, dtype=F32) / n_freq)
    ang_r = (tok // GRID_W)[:, None].astype(F32) * inv
    ang_c = (tok % GRID_W)[:, None].astype(F32) * inv
    cos64 = jnp.concatenate([jnp.cos(ang_r)] * 2 + [jnp.cos(ang_c)] * 2, axis=-1)
    sin64 = jnp.concatenate([-jnp.sin(ang_r), jnp.sin(ang_r), -jnp.sin(ang_c), jnp.sin(ang_c)], axis=-1)
    return jnp.tile(cos64, (1, 2)), jnp.tile(sin64, (1, 2))


def kernel(x, c, ctx, c_ctx, w_ada, b_ada, g_norm_mix, g_norm_ffn, w_in, w_dw, b_dw, ln_g_conv, ln_b_conv,
           w_conv_out, b_conv_out, lambda_q1, lambda_k1, lambda_q2, lambda_k2, g_subln, w_attn_out, w_out,
           w_router, w_expert_gate, w_expert_up, w_expert_down, g_final):
    bsz, t, d = x.shape
    n_ctx = ctx.shape[1]
    depth = w_ada.shape[0]
    assert depth == 1, "single-layer trunk: the context stream only feeds keys/values"
    ne = w_router.shape[2]
    cap = CAPACITY_FACTOR * t // ne
    dc = w_dw.shape[2]
    qkw = N_HEADS * 2 * HEAD_DIM
    aw = N_HEADS * V_DIM
    assert dc == d and qkw == d and aw == d, "projection groups are addressed as equal-width column blocks"
    layer = 0
    lam_init = 0.8 - 0.6 * math.exp(-0.3 * layer)
    lam = (jnp.exp(jnp.sum(lambda_q1[layer].astype(F32) * lambda_k1[layer].astype(F32)))
           - jnp.exp(jnp.sum(lambda_q2[layer].astype(F32) * lambda_k2[layer].astype(F32)))
           + lam_init).reshape(1)

    row = lambda v: v.reshape(1, -1)
    x2 = x.reshape(bsz * t, d)
    ctx2 = ctx.reshape(bsz * n_ctx, d)

    pad = (-(bsz + 1)) % 8
    cc = jnp.concatenate([c, c_ctx[None, :], jnp.zeros((pad, d), F32)], axis=0)
    mod = _adaln(cc, w_ada[layer], row(b_ada[layer]))
    mod_lat = mod[:bsz].reshape(bsz, N_MOD, d)
    mod_ctx = mod[bsz:bsz + 1].reshape(1, N_MOD, d)

    w_in_bf = w_in[layer].astype(BF16)
    g_mix = row(g_norm_mix[layer])
    cos, sin_signed = _rope_tables(t)

    u, qkv, gates = _proj_all(x2, mod_lat, g_mix, w_in_bf, cos, sin_signed, t)
    kvc = _proj_ctx(ctx2, mod_ctx, g_mix, w_in_bf, 3)

    gy = _conv_branch(u.reshape(bsz, t, dc), gates, w_dw[layer], row(b_dw[layer]), row(ln_g_conv[layer]),
                      row(ln_b_conv[layer]), w_conv_out[layer].astype(BF16), row(b_conv_out[layer]))
    o = _attention(lam, qkv, kvc, row(g_subln[layer]), bsz, t, n_ctx, 1.0 - lam_init)

    xmid, hf, aff_t = _post(o, gates, gy, x2, mod_lat, w_attn_out[layer].astype(BF16), w_out[layer].astype(BF16),
                            row(g_norm_ffn[layer]), w_router[layer].T, bsz, t)

    pos, rank = _route(aff_t.reshape(bsz, ne, t // LANES, LANES), cap)
    tc = _tile(t, 4 * LANES)
    chunk_slots = jnp.concatenate([rank[:, :, ::tc // LANES, 0], jnp.full((bsz, ne, 1), cap, jnp.int32)],
                                  axis=-1).reshape(-1)
    xs, gate = _gather(chunk_slots, pos.reshape(bsz, ne, 1, t), aff_t.reshape(bsz, ne, 1, t),
                       hf.reshape(bsz, t, d), cap, tc)
    ys = _experts(xs.reshape(ne, bsz * cap, d), gate.reshape(ne, bsz * cap, 1),
                  w_expert_gate[layer], w_expert_up[layer], w_expert_down[layer])
    out = _combine(chunk_slots, pos.reshape(bsz, ne, t), ys.reshape(ne, bsz, cap, d), xmid, mod_lat,
                   row(g_final), cap, tc)
    return out.reshape(bsz, t, d)
```

```python
import functools
import math

import jax
import jax.numpy as jnp
from jax import lax
from jax.experimental import pallas as pl
from jax.experimental.pallas import tpu as pltpu

N_HEADS = 8
HEAD_DIM = 64
V_DIM = 2 * HEAD_DIM
CONV_WIDTH = 31
CONV_HALF = CONV_WIDTH // 2
N_MOD = 6
GRID_W = 64
ROPE_THETA = 10000.0
CAPACITY_FACTOR = 2
EPS = 1e-6

LANES = 128
SUBLANES = 8
HALO = 16
VMEM_LIMIT = 56 * 1024 * 1024

F32 = jnp.float32
BF16 = jnp.bfloat16


def _cparams(sem):
    return pltpu.CompilerParams(dimension_semantics=sem, vmem_limit_bytes=VMEM_LIMIT)


def _tile(n, pref):
    t = min(n, pref)
    while n % t:
        t //= 2
    return t


def _split_bf16(a):
    hi = a.astype(BF16)
    lo = (a - hi.astype(F32)).astype(BF16)
    return hi, lo


def _dot3(a, b, dims):
    ah, al = _split_bf16(a)
    bh, bl = _split_bf16(b)
    dg = functools.partial(lax.dot_general, dimension_numbers=dims, preferred_element_type=F32)
    return dg(ah, bh) + (dg(ah, bl) + dg(al, bh))


_NN = (((1,), (0,)), ((), ()))
_NT = (((1,), (1,)), ((), ()))


def _sigmoid(x):
    return 1.0 / (1.0 + jnp.exp(-x))


def _adaln_kernel(c_ref, w_ref, b_ref, o_ref):
    c = c_ref[...]
    s = c * _sigmoid(c)
    o_ref[...] = _dot3(s, w_ref[...], _NN) + b_ref[...]


def _adaln(cc, w, b):
    rows, d = cc.shape
    n = w.shape[1]
    tn = _tile(n, 1024)
    return pl.pallas_call(
        _adaln_kernel,
        grid=(n // tn,),
        in_specs=[pl.BlockSpec((rows, d), lambda j: (0, 0)),
                  pl.BlockSpec((d, tn), lambda j: (0, j)),
                  pl.BlockSpec((1, tn), lambda j: (0, j))],
        out_specs=pl.BlockSpec((rows, tn), lambda j: (0, j)),
        out_shape=jax.ShapeDtypeStruct((rows, n), F32),
        compiler_params=_cparams(("arbitrary",)),
        name="adaln",
    )(cc, w, b)


def _norm_mod(x_ref, mod_ref, g_ref):
    x = x_ref[...]
    ms = jnp.mean(x * x, axis=-1, keepdims=True)
    y = x * lax.rsqrt(ms + EPS) * g_ref[...]
    return (y * (1.0 + mod_ref[0, 1:2, :]) + mod_ref[0, 0:1, :]).astype(BF16)


def _rope(acc, cos, sin_signed):
    lane = lax.broadcasted_iota(jnp.int32, (acc.shape[0], LANES), 1)
    first_half = (lane % 32) < 16
    outs = []
    for hh in range(acc.shape[1] // LANES):
        seg = acc[:, hh * LANES:(hh + 1) * LANES]
        partner = jnp.where(first_half, pltpu.roll(seg, LANES - 16, 1), pltpu.roll(seg, 16, 1))
        outs.append(seg * cos + partner * sin_signed)
    return jnp.concatenate(outs, axis=1)


def _proj_ctx_kernel(x_ref, mod_ref, g_ref, wk_ref, wv_ref, o_ref):
    h = _norm_mod(x_ref, mod_ref, g_ref)
    o_ref[0] = jnp.dot(h, wk_ref[...], preferred_element_type=F32).astype(o_ref.dtype)
    o_ref[1] = jnp.dot(h, wv_ref[...], preferred_element_type=F32).astype(o_ref.dtype)


def _proj_ctx(x2, mod3, g, w_bf, k_group):
    m, d = x2.shape
    tm = _tile(m, 512)
    return pl.pallas_call(
        _proj_ctx_kernel,
        grid=(m // tm,),
        in_specs=[pl.BlockSpec((tm, d), lambda i: (i, 0)),
                  pl.BlockSpec((1, N_MOD, d), lambda i: (0, 0, 0)),
                  pl.BlockSpec((1, d), lambda i: (0, 0)),
                  pl.BlockSpec((d, d), lambda i: (0, k_group)),
                  pl.BlockSpec((d, d), lambda i: (0, k_group + 1))],
        out_specs=pl.BlockSpec((2, tm, d), lambda i: (0, i, 0)),
        out_shape=jax.ShapeDtypeStruct((2, m, d), BF16),
        compiler_params=_cparams(("parallel",)),
        name="proj_ctx",
    )(x2, mod3, g, w_bf, w_bf)


def _proj_all_kernel(x_ref, mod_ref, g_ref, w_ref, cos_ref, sin_ref, u_ref, qkv_ref, gate_ref):
    d = x_ref.shape[1]
    h = _norm_mod(x_ref, mod_ref, g_ref)

    def group(n):
        return jnp.dot(h, w_ref[:, n * d:(n + 1) * d], preferred_element_type=F32)

    cos = cos_ref[...]
    sin = sin_ref[...]
    u_ref[...] = (group(0) * _sigmoid(group(1))).astype(u_ref.dtype)
    qkv_ref[0] = _rope(group(2) * (HEAD_DIM ** -0.5 * math.log2(math.e)), cos, sin).astype(qkv_ref.dtype)
    qkv_ref[1] = _rope(group(3), cos, sin).astype(qkv_ref.dtype)
    qkv_ref[2] = group(4).astype(qkv_ref.dtype)
    gate_ref[0] = _sigmoid(group(5)).astype(gate_ref.dtype)
    gate_ref[1] = _sigmoid(group(6)).astype(gate_ref.dtype)


def _proj_all(x2, mod3, g, w_bf, cos, sin_signed, t):
    m, d = x2.shape
    tm = _tile(t, 512)
    tblocks = t // tm
    return pl.pallas_call(
        _proj_all_kernel,
        grid=(m // tm,),
        in_specs=[pl.BlockSpec((tm, d), lambda i: (i, 0)),
                  pl.BlockSpec((1, N_MOD, d), lambda i: (i // tblocks, 0, 0)),
                  pl.BlockSpec((1, d), lambda i: (0, 0)),
                  pl.BlockSpec(w_bf.shape, lambda i: (0, 0), pipeline_mode=pl.Buffered(1)),
                  pl.BlockSpec((tm, LANES), lambda i: (i % tblocks, 0)),
                  pl.BlockSpec((tm, LANES), lambda i: (i % tblocks, 0))],
        out_specs=[pl.BlockSpec((tm, d), lambda i: (i, 0)),
                   pl.BlockSpec((3, tm, d), lambda i: (0, i, 0)),
                   pl.BlockSpec((2, tm, d), lambda i: (0, i, 0))],
        out_shape=[jax.ShapeDtypeStruct((m, d), BF16),
                   jax.ShapeDtypeStruct((3, m, d), BF16),
                   jax.ShapeDtypeStruct((2, m, d), BF16)],
        compiler_params=_cparams(("parallel",)),
        name="proj_all",
    )(x2, mod3, g, w_bf, cos, sin_signed)


def _conv_kernel(up_ref, um_ref, un_ref, gc_ref, wdw_ref, bdw_ref, lng_ref, lnb_ref, wc_ref, bc_ref,
                 o_ref, buf, cv, *, tt, rb):
    ti = pl.program_id(1)
    nt = pl.num_programs(1)
    d = um_ref.shape[2]
    prev = up_ref[0].astype(F32)
    nxt = un_ref[0].astype(F32)
    buf[0:HALO, :] = jnp.where(ti > 0, prev, 0.0)
    buf[HALO:HALO + tt, :] = um_ref[0].astype(F32)
    buf[HALO + tt:2 * HALO + tt, :] = jnp.where(ti < nt - 1, nxt, 0.0)

    off = HALO - CONV_HALF
    taps = [[(s - off, s // SUBLANES) for s in range(off, off + CONV_WIDTH) if s % SUBLANES == r]
            for r in range(SUBLANES)]
    n_slab = (off + CONV_WIDTH - 1) // SUBLANES + 1
    for c in range(d // LANES):
        cs = slice(c * LANES, (c + 1) * LANES)
        for rblk in range(tt // rb):
            r0 = rblk * rb
            slabs = [buf[r0 + SUBLANES * a:r0 + SUBLANES * a + rb + SUBLANES, cs] for a in range(n_slab)]
            acc = None
            for r in range(SUBLANES):
                z = None
                for k, a in taps[r]:
                    term = slabs[a] * wdw_ref[k:k + 1, cs]
                    z = term if z is None else z + term
                z = z[r:r + rb]
                acc = z if acc is None else acc + z
            cv[r0:r0 + rb, cs] = acc

    v = cv[...] + bdw_ref[...]
    mu = jnp.mean(v, axis=-1, keepdims=True)
    xc = v - mu
    var = jnp.mean(xc * xc, axis=-1, keepdims=True)
    y = xc * lax.rsqrt(var + EPS) * lng_ref[...] + lnb_ref[...]
    z = y * _sigmoid(y)
    yc = jnp.dot(z.astype(BF16), wc_ref[...], preferred_element_type=F32) + bc_ref[...]
    o_ref[...] = (gc_ref[0].astype(F32) * yc).astype(o_ref.dtype)


def _conv_branch(u3, gates, wdw, bdw, lng, lnb, wc_bf, bc):
    bsz, t, dc = u3.shape
    d = wc_bf.shape[1]
    tt = _tile(t, 256)
    nt = t // tt
    hb = tt // HALO
    nhb = t // HALO
    kern = functools.partial(_conv_kernel, tt=tt, rb=_tile(tt, 64))
    const = lambda shape: pl.BlockSpec(shape, lambda b, i: (0,) * len(shape))
    return pl.pallas_call(
        kern,
        grid=(bsz, nt),
        in_specs=[pl.BlockSpec((1, HALO, dc), lambda b, i: (b, jnp.maximum(i * hb - 1, 0), 0)),
                  pl.BlockSpec((1, tt, dc), lambda b, i: (b, i, 0)),
                  pl.BlockSpec((1, HALO, dc), lambda b, i: (b, jnp.minimum((i + 1) * hb, nhb - 1), 0)),
                  pl.BlockSpec((1, tt, d), lambda b, i: (0, b * nt + i, 0)),
                  const((CONV_WIDTH, dc)), const((1, dc)), const((1, dc)), const((1, dc)),
                  const((dc, d)), const((1, d))],
        out_specs=pl.BlockSpec((tt, d), lambda b, i: (b * nt + i, 0)),
        out_shape=jax.ShapeDtypeStruct((bsz * t, d), BF16),
        scratch_shapes=[pltpu.VMEM((tt + 2 * HALO, dc), F32), pltpu.VMEM((tt, dc), F32)],
        compiler_params=_cparams(("parallel", "parallel")),
        name="conv",
    )(u3, u3, u3, gates, wdw, bdw, lng, lnb, wc_bf, bc)


def _attn_kernel(lam_ref, q_ref, kl_ref, vl_ref, kc_ref, vc_ref, gs_ref, o_ref, vext, s_scr, m_scr,
                 *, tq, kb, nq, n_tiles, out_scale):
    n = pl.program_id(0)
    t = kl_ref.shape[1]
    pair = lax.div(n, nq)
    prev_pair = lax.div(jnp.maximum(n - 1, 0), nq)

    @pl.when((lax.rem(n, nq) == 0) & (n < n_tiles))
    def _():
        slot = lax.rem(pair, 2)
        vext[slot, 0:V_DIM, 0:t] = vl_ref[0].astype(F32).T.astype(vext.dtype)
        vext[slot, 0:V_DIM, t:] = vc_ref[0].astype(F32).T.astype(vext.dtype)
        row = lax.broadcasted_iota(jnp.int32, (vext.shape[1] - V_DIM, vext.shape[2]), 0)
        vext[slot, V_DIM:, :] = jnp.where(row == 0, 1.0, 0.0).astype(vext.dtype)

    @pl.when(n == 0)
    def _():
        s_scr[...] = jnp.zeros_like(s_scr)
        m_scr[...] = jnp.zeros_like(m_scr)

    q = q_ref[0]
    lane = lax.broadcasted_iota(jnp.int32, q.shape, 1)
    zero = jnp.zeros_like(q)
    qs = jnp.concatenate([jnp.where(lane < HEAD_DIM, q, zero), jnp.where(lane >= HEAD_DIM, q, zero)], axis=0)
    qs_t = qs.astype(F32).T.astype(BF16)
    vprev = vext.at[lax.rem(prev_pair, 2)]
    m_old = m_scr[...]
    acc = None
    m_new = None
    blocks = [(kl_ref, r, kb) for r in range(0, t, kb)] + [(kc_ref, 0, kc_ref.shape[1])]
    row = 0
    for k_ref, r0, size in blocks:
        rows = slice(row, row + size)
        p = jnp.exp2(s_scr[rows, :] - m_old).astype(BF16)
        part = jnp.dot(vprev[:, rows], p, preferred_element_type=F32)
        acc = part if acc is None else acc + part
        s_blk = jnp.dot(k_ref[0, r0:r0 + size, :], qs_t, preferred_element_type=F32)
        s_scr[rows, :] = s_blk
        blk_max = jnp.max(s_blk, axis=0, keepdims=True)
        m_new = blk_max if m_new is None else jnp.maximum(m_new, blk_max)
        row += size
    m_scr[...] = m_new

    o = acc[:V_DIM] / acc[V_DIM:V_DIM + 1]
    o = o[:, :tq] - lam_ref[0] * o[:, tq:]
    ms = jnp.mean(o * o, axis=0, keepdims=True)
    o = o * lax.rsqrt(ms + EPS) * gs_ref[...] * out_scale
    o_ref[...] = o.T.astype(o_ref.dtype)


def _attention(lam, qkv, kvc, g_subln, bsz, t, n_ctx, out_scale):
    m, width = qkv.shape[1], qkv.shape[2]
    nh = width // V_DIM
    tq = _tile(t, 512)
    nq = t // tq
    s_all = t + n_ctx
    n_tiles = bsz * nh * nq
    kern = functools.partial(_attn_kernel, tq=tq, kb=_tile(t, 256), nq=nq, n_tiles=n_tiles, out_scale=out_scale)

    def tile_coords(n):
        pair = lax.div(n, nq)
        return lax.div(pair, nh), lax.rem(pair, nh), lax.rem(n, nq)

    def started(n):
        return tile_coords(jnp.minimum(n, n_tiles - 1))

    def finished(n):
        return tile_coords(jnp.maximum(n - 1, 0))

    def q_map(n):
        b, h, j = started(n)
        return 0, b * nq + j, h

    def kv_map(which):
        def index(n):
            b, h, _ = started(n)
            return which, b, h
        return index

    def out_map(n):
        b, h, j = finished(n)
        return b * nq + j, h

    return pl.pallas_call(
        kern,
        grid=(n_tiles + 1,),
        in_specs=[pl.BlockSpec(memory_space=pltpu.SMEM),
                  pl.BlockSpec((1, tq, V_DIM), q_map),
                  pl.BlockSpec((1, t, V_DIM), kv_map(1)),
                  pl.BlockSpec((1, t, V_DIM), kv_map(2)),
                  pl.BlockSpec((1, n_ctx, V_DIM), kv_map(0)),
                  pl.BlockSpec((1, n_ctx, V_DIM), kv_map(1)),
                  pl.BlockSpec((V_DIM, 1), lambda n: (0, 0))],
        out_specs=pl.BlockSpec((tq, V_DIM), out_map),
        out_shape=jax.ShapeDtypeStruct((m, width), BF16),
        scratch_shapes=[pltpu.VMEM((2, V_DIM + 2 * SUBLANES, s_all), BF16),
                        pltpu.VMEM((s_all, 2 * tq), F32),
                        pltpu.VMEM((1, 2 * tq), F32)],
        compiler_params=_cparams(("arbitrary",)),
        name="attn",
    )(lam, qkv, qkv, qkv, kvc, kvc, g_subln.reshape(V_DIM, 1))


def _post_kernel(o_ref, ga_ref, gy_ref, x_ref, mod_ref, wa_ref, wo_ref, gf_ref, wrt_ref,
                 xmid_ref, hf_ref, aff_ref):
    y_attn = jnp.dot(o_ref[...], wa_ref[...], preferred_element_type=F32)
    merged = ga_ref[0].astype(F32) * y_attn + gy_ref[...].astype(F32)
    mix = jnp.dot(merged.astype(BF16), wo_ref[...], preferred_element_type=F32)
    xm = x_ref[...] + mod_ref[0, 2:3, :] * mix
    xmid_ref[...] = xm
    ms = jnp.mean(xm * xm, axis=-1, keepdims=True)
    hf = xm * lax.rsqrt(ms + EPS) * gf_ref[...]
    hf = hf * (1.0 + mod_ref[0, 4:5, :]) + mod_ref[0, 3:4, :]
    hf_ref[...] = hf.astype(hf_ref.dtype)
    logits = _dot3(wrt_ref[...], hf, _NT)
    z = jnp.exp(logits - jnp.max(logits, axis=0, keepdims=True))
    aff_ref[0] = z / jnp.sum(z, axis=0, keepdims=True)


def _post(o, gates, gy, x2, mod3, wa_bf, wo_bf, gf, wrt, bsz, t):
    m, d = x2.shape
    aw = o.shape[1]
    ne = wrt.shape[0]
    tm = _tile(t, 512)
    nt = t // tm
    const = lambda shape: pl.BlockSpec(shape, lambda i: (0,) * len(shape))
    return pl.pallas_call(
        _post_kernel,
        grid=(m // tm,),
        in_specs=[pl.BlockSpec((tm, aw), lambda i: (i, 0)),
                  pl.BlockSpec((1, tm, d), lambda i: (1, i, 0)),
                  pl.BlockSpec((tm, d), lambda i: (i, 0)),
                  pl.BlockSpec((tm, d), lambda i: (i, 0)),
                  pl.BlockSpec((1, N_MOD, d), lambda i: (i // nt, 0, 0)),
                  const((aw, d)), const((d, d)), const((1, d)), const((ne, d))],
        out_specs=[pl.BlockSpec((tm, d), lambda i: (i, 0)),
                   pl.BlockSpec((tm, d), lambda i: (i, 0)),
                   pl.BlockSpec((1, ne, tm), lambda i: (i // nt, 0, i % nt))],
        out_shape=[jax.ShapeDtypeStruct((m, d), F32),
                   jax.ShapeDtypeStruct((m, d), BF16),
                   jax.ShapeDtypeStruct((bsz, ne, t), F32)],
        compiler_params=_cparams(("parallel",)),
        name="post",
    )(o, gates, gy, x2, mod3, wa_bf, wo_bf, gf, wrt)


def _count(mask):
    ones = jnp.where(mask, 1.0, 0.0)
    return jnp.sum(jnp.sum(ones, axis=2, keepdims=True), axis=1, keepdims=True)


def _route_kernel(a_ref, pos_ref, rank_ref, *, cap):
    a = a_ref[0]
    ne, nc, ln = a.shape
    rows = ne * nc
    bits = lax.bitcast_convert_type(a, jnp.int32)

    def search(i, cur):
        cand = cur | jnp.left_shift(jnp.int32(1), 30 - i)
        return jnp.where(_count(bits >= cand) >= cap, cand, cur)

    tau = lax.fori_loop(0, 31, search, jnp.zeros((ne, 1, 1), jnp.int32))

    r_i = lax.broadcasted_iota(jnp.int32, (rows, rows), 0)
    r_j = lax.broadcasted_iota(jnp.int32, (rows, rows), 1)
    shift = int(math.log2(nc))
    same_expert = lax.shift_right_logical(r_i, shift) == lax.shift_right_logical(r_j, shift)
    chunk_before = jnp.where(same_expert & (r_j < r_i), 1.0, 0.0).astype(BF16)
    l_i = lax.broadcasted_iota(jnp.int32, (ln, ln), 0)
    l_j = lax.broadcasted_iota(jnp.int32, (ln, ln), 1)
    lane_upto = jnp.where(l_i <= l_j, 1.0, 0.0).astype(BF16)

    def excl_prefix(mask):
        xm = jnp.where(mask, 1.0, 0.0).reshape(rows, ln)
        incl = jnp.dot(xm.astype(BF16), lane_upto, preferred_element_type=F32)
        tot = jnp.broadcast_to(incl[:, ln - 1:ln], (rows, ln))
        offs = jnp.dot(chunk_before, tot.astype(BF16), preferred_element_type=F32)
        return (incl + offs - xm).reshape(ne, nc, ln)

    gt = bits > tau
    eq = bits == tau
    need = cap - _count(gt)
    sel = gt | (eq & (excl_prefix(eq) < need))
    pos = excl_prefix(sel)
    pos_ref[0] = jnp.where(sel, pos, -1.0).astype(jnp.int32)
    rank_ref[0] = pos.astype(jnp.int32)


def _route(aff4, cap):
    bsz, ne, nc, ln = aff4.shape
    assert nc & (nc - 1) == 0, "token chunks per sample must be a power of two"
    spec = pl.BlockSpec((1, ne, nc, ln), lambda b: (b, 0, 0, 0))
    return pl.pallas_call(
        functools.partial(_route_kernel, cap=cap),
        grid=(bsz,),
        in_specs=[spec],
        out_specs=[spec, spec],
        out_shape=[jax.ShapeDtypeStruct((bsz, ne, nc, ln), jnp.int32)] * 2,
        compiler_params=_cparams(("parallel",)),
        name="route",
    )(aff4)


def _slot_range(cs_ref, b, e, ne, nchunk, c):
    base = (b * ne + e) * (nchunk + 1) + c
    lo, hi = cs_ref[base], cs_ref[base + 1]
    return lax.shift_right_logical(lo, 4) * 16, hi


def _max_windows(starts_and_ends, win):
    n = None
    for start, end in starts_and_ends:
        n_e = lax.div(end - start + (win - 1), win)
        n = n_e if n is None else jnp.maximum(n, n_e)
    return n


def _gather_kernel(cs_ref, pos_ref, aff_ref, hf_ref, xs_ref, gate_ref, acc, gacc, *, cap, tc, win):
    b, eg = pl.program_id(0), pl.program_id(1)
    group = pos_ref.shape[1]
    ne = pl.num_programs(1) * group
    nchunk = hf_ref.shape[1] // tc
    acc[...] = jnp.zeros_like(acc)
    gacc[...] = jnp.zeros_like(gacc)
    for c in range(nchunk):
        ranges = [_slot_range(cs_ref, b, eg * group + g, ne, nchunk, c) for g in range(group)]
        cols = slice(c * tc, (c + 1) * tc)

        def window(w, carry):
            rows = [pl.multiple_of(jnp.minimum(r0 + w * win, cap), 16) for r0, _ in ranges]
            hits = [pos_ref[0, g, :, cols] == r + lax.broadcasted_iota(jnp.int32, (win, tc), 0)
                    for g, r in enumerate(rows)]
            onehot = jnp.concatenate([jnp.where(h, 1.0, 0.0).astype(BF16) for h in hits], axis=0)
            part = jnp.dot(onehot, hf_ref[0, cols, :], preferred_element_type=F32)
            for g, r in enumerate(rows):
                acc[g, pl.ds(r, win), :] += part[g * win:(g + 1) * win]
                gacc[g, pl.ds(r, win), :] += jnp.sum(jnp.where(hits[g], aff_ref[0, g, :, cols], 0.0),
                                                     axis=1, keepdims=True)
            return carry

        lax.fori_loop(0, _max_windows(ranges, win), window, 0)
    for g in range(group):
        xs_ref[g, 0] = acc[g, 0:cap, :].astype(xs_ref.dtype)
        gate_ref[g, 0] = gacc[g, 0:cap, :]


def _gather(chunk_slots, pos4, aff4, hf3, cap, tc):
    bsz, ne, _, t = pos4.shape
    d = hf3.shape[2]
    win = _tile(cap, 128)
    group = _tile(ne, 4)
    kern = functools.partial(_gather_kernel, cap=cap, tc=tc, win=win)
    return pl.pallas_call(
        kern,
        grid_spec=pltpu.PrefetchScalarGridSpec(
            num_scalar_prefetch=1,
            grid=(bsz, ne // group),
            in_specs=[pl.BlockSpec((1, group, 1, t), lambda b, e, cs: (b, e, 0, 0)),
                      pl.BlockSpec((1, group, 1, t), lambda b, e, cs: (b, e, 0, 0)),
                      pl.BlockSpec((1, t, d), lambda b, e, cs: (b, 0, 0))],
            out_specs=[pl.BlockSpec((group, 1, cap, d), lambda b, e, cs: (e, b, 0, 0)),
                       pl.BlockSpec((group, 1, cap, 1), lambda b, e, cs: (e, b, 0, 0))],
            scratch_shapes=[pltpu.VMEM((group, cap + win, d), F32), pltpu.VMEM((group, cap + win, 1), F32)]),
        out_shape=[jax.ShapeDtypeStruct((ne, bsz, cap, d), BF16),
                   jax.ShapeDtypeStruct((ne, bsz, cap, 1), F32)],
        compiler_params=_cparams(("parallel", "arbitrary")),
        name="gather",
    )(chunk_slots, pos4, aff4, hf3)


def _expert_kernel(xs_ref, gate_ref, wg_ref, wu_ref, wd_ref, o_ref, acc):
    f = pl.program_id(2)

    @pl.when(f == 0)
    def _():
        acc[...] = jnp.zeros_like(acc)

    x = xs_ref[0]
    g = jnp.dot(x, wg_ref[0].astype(BF16), preferred_element_type=F32)
    u = jnp.dot(x, wu_ref[0].astype(BF16), preferred_element_type=F32)
    hid = (g * _sigmoid(g)) * u
    acc[...] += jnp.dot(hid.astype(BF16), wd_ref[0].astype(BF16), preferred_element_type=F32)

    @pl.when(f == pl.num_programs(2) - 1)
    def _():
        o_ref[0] = (acc[...] * gate_ref[0]).astype(o_ref.dtype)


def _experts(xs3, gate3, wg, wu, wd):
    ne, r, d = xs3.shape
    ff = wg.shape[2]
    tr = _tile(r, 2048)
    tf = _tile(ff, 256)
    return pl.pallas_call(
        _expert_kernel,
        grid=(ne, r // tr, ff // tf),
        in_specs=[pl.BlockSpec((1, tr, d), lambda e, i, f: (e, i, 0)),
                  pl.BlockSpec((1, tr, 1), lambda e, i, f: (e, i, 0)),
                  pl.BlockSpec((1, d, tf), lambda e, i, f: (e, 0, f)),
                  pl.BlockSpec((1, d, tf), lambda e, i, f: (e, 0, f)),
                  pl.BlockSpec((1, tf, d), lambda e, i, f: (e, f, 0))],
        out_specs=pl.BlockSpec((1, tr, d), lambda e, i, f: (e, i, 0)),
        out_shape=jax.ShapeDtypeStruct((ne, r, d), BF16),
        scratch_shapes=[pltpu.VMEM((tr, d), F32)],
        compiler_params=_cparams(("parallel", "parallel", "arbitrary")),
        name="experts",
    )(xs3, gate3, wg, wu, wd)


def _combine_kernel(cs_ref, pos_ref, ys_ref, xm_ref, mod_ref, gfin_ref, o_ref, acc, pos_t, *, cap, win, group):
    b, i = pl.program_id(0), pl.program_id(1)
    nchunk = pl.num_programs(1)
    ne, tt = pos_ref.shape[1], pos_ref.shape[2]
    p = pos_ref[0].astype(F32)
    padded = jnp.concatenate([p, jnp.full((LANES - ne, tt), -1.0, F32)], axis=0)
    pos_t[...] = padded.T
    acc[...] = jnp.zeros_like(acc)
    for e0 in range(0, ne, group):
        experts = range(e0, e0 + group)
        ranges = [_slot_range(cs_ref, b, e, ne, nchunk, i) for e in experts]

        def window(w, carry):
            onehots, outputs = [], []
            for e, (k0, _) in zip(experts, ranges):
                first = k0 + w * win
                k = pl.multiple_of(jnp.minimum(first, cap - win), 16)
                slot = k + lax.broadcasted_iota(jnp.int32, (tt, win), 1)
                mine = (pos_t[:, e:e + 1] == slot.astype(F32)) & (slot >= first)
                onehots.append(jnp.where(mine, 1.0, 0.0).astype(BF16))
                outputs.append(ys_ref[e, 0, pl.ds(k, win), :])
            acc[...] += jnp.dot(jnp.concatenate(onehots, axis=1), jnp.concatenate(outputs, axis=0),
                                preferred_element_type=F32)
            return carry

        lax.fori_loop(0, _max_windows(ranges, win), window, 0)
    xo = xm_ref[...] + mod_ref[0, 5:6, :] * acc[...]
    ms = jnp.mean(xo * xo, axis=-1, keepdims=True)
    o_ref[...] = xo * lax.rsqrt(ms + EPS) * gfin_ref[...]


def _combine(chunk_slots, pos3, ys4, xmid, mod3, gfin, cap, tc):
    bsz, ne, t = pos3.shape
    m, d = xmid.shape
    nt = t // tc
    return pl.pallas_call(
        functools.partial(_combine_kernel, cap=cap, win=_tile(cap, 128), group=_tile(ne, 4)),
        grid_spec=pltpu.PrefetchScalarGridSpec(
            num_scalar_prefetch=1,
            grid=(bsz, nt),
            in_specs=[pl.BlockSpec((1, ne, tc), lambda b, i, cs: (b, 0, i)),
                      pl.BlockSpec((ne, 1, cap, d), lambda b, i, cs: (0, b, 0, 0), pipeline_mode=pl.Buffered(1)),
                      pl.BlockSpec((tc, d), lambda b, i, cs: (b * nt + i, 0)),
                      pl.BlockSpec((1, N_MOD, d), lambda b, i, cs: (b, 0, 0)),
                      pl.BlockSpec((1, d), lambda b, i, cs: (0, 0))],
            out_specs=pl.BlockSpec((tc, d), lambda b, i, cs: (b * nt + i, 0)),
            scratch_shapes=[pltpu.VMEM((tc, d), F32), pltpu.VMEM((tc, LANES), F32)]),
        out_shape=jax.ShapeDtypeStruct((m, d), F32),
        compiler_params=_cparams(("parallel", "arbitrary")),
        name="combine",
    )(chunk_slots, pos3, ys4, xmid, mod3, gfin)


def _rope_tables(t):
    n_freq = HEAD_DIM // 4
    tok = jnp.arange(t)
    inv = ROPE_THETA ** (-jnp.arange(n_freq, dtype=F32) / n_freq)
    ang_r = (tok // GRID_W)[:, None].astype(F32) * inv
    ang_c = (tok % GRID_W)[:, None].astype(F32) * inv
    cos64 = jnp.concatenate([jnp.cos(ang_r)] * 2 + [jnp.cos(ang_c)] * 2, axis=-1)
    sin64 = jnp.concatenate([-jnp.sin(ang_r), jnp.sin(ang_r), -jnp.sin(ang_c), jnp.sin(ang_c)], axis=-1)
    return jnp.tile(cos64, (1, 2)), jnp.tile(sin64, (1, 2))


def kernel(x, c, ctx, c_ctx, w_ada, b_ada, g_norm_mix, g_norm_ffn, w_in, w_dw, b_dw, ln_g_conv, ln_b_conv,
           w_conv_out, b_conv_out, lambda_q1, lambda_k1, lambda_q2, lambda_k2, g_subln, w_attn_out, w_out,
           w_router, w_expert_gate, w_expert_up, w_expert_down, g_final):
    bsz, t, d = x.shape
    n_ctx = ctx.shape[1]
    depth = w_ada.shape[0]
    assert depth == 1, "single-layer trunk: the context stream only feeds keys/values"
    ne = w_router.shape[2]
    cap = CAPACITY_FACTOR * t // ne
    dc = w_dw.shape[2]
    qkw = N_HEADS * 2 * HEAD_DIM
    aw = N_HEADS * V_DIM
    assert dc == d and qkw == d and aw == d, "projection groups are addressed as equal-width column blocks"
    layer = 0
    lam_init = 0.8 - 0.6 * math.exp(-0.3 * layer)
    lam = (jnp.exp(jnp.sum(lambda_q1[layer].astype(F32) * lambda_k1[layer].astype(F32)))
           - jnp.exp(jnp.sum(lambda_q2[layer].astype(F32) * lambda_k2[layer].astype(F32)))
           + lam_init).reshape(1)

    row = lambda v: v.reshape(1, -1)
    x2 = x.reshape(bsz * t, d)
    ctx2 = ctx.reshape(bsz * n_ctx, d)

    pad = (-(bsz + 1)) % 8
    cc = jnp.concatenate([c, c_ctx[None, :], jnp.zeros((pad, d), F32)], axis=0)
    mod = _adaln(cc, w_ada[layer], row(b_ada[layer]))
    mod_lat = mod[:bsz].reshape(bsz, N_MOD, d)
    mod_ctx = mod[bsz:bsz + 1].reshape(1, N_MOD, d)

    w_in_bf = w_in[layer].astype(BF16)
    g_mix = row(g_norm_mix[layer])
    cos, sin_signed = _rope_tables(t)

    u, qkv, gates = _proj_all(x2, mod_lat, g_mix, w_in_bf, cos, sin_signed, t)
    kvc = _proj_ctx(ctx2, mod_ctx, g_mix, w_in_bf, 3)

    gy = _conv_branch(u.reshape(bsz, t, dc), gates, w_dw[layer], row(b_dw[layer]), row(ln_g_conv[layer]),
                      row(ln_b_conv[layer]), w_conv_out[layer].astype(BF16), row(b_conv_out[layer]))
    o = _attention(lam, qkv, kvc, row(g_subln[layer]), bsz, t, n_ctx, 1.0 - lam_init)

    xmid, hf, aff_t = _post(o, gates, gy, x2, mod_lat, w_attn_out[layer].astype(BF16), w_out[layer].astype(BF16),
                            row(g_norm_ffn[layer]), w_router[layer].T, bsz, t)

    pos, rank = _route(aff_t.reshape(bsz, ne, t // LANES, LANES), cap)
    tc = _tile(t, 4 * LANES)
    chunk_slots = jnp.concatenate([rank[:, :, ::tc // LANES, 0], jnp.full((bsz, ne, 1), cap, jnp.int32)],
                                  axis=-1).reshape(-1)
    xs, gate = _gather(chunk_slots, pos.reshape(bsz, ne, 1, t), aff_t.reshape(bsz, ne, 1, t),
                       hf.reshape(bsz, t, d), cap, tc)
    ys = _experts(xs.reshape(ne, bsz * cap, d), gate.reshape(ne, bsz * cap, 1),
                  w_expert_gate[layer], w_expert_up[layer], w_expert_down[layer])
    out = _combine(chunk_slots, pos.reshape(bsz, ne, t), ys.reshape(ne, bsz, cap, d), xmid, mod_lat,
                   row(g_final), cap, tc)
    return out.reshape(bsz, t, d)
```

```python
import functools
import math

import jax
import jax.numpy as jnp
from jax import lax
from jax.experimental import pallas as pl
from jax.experimental.pallas import tpu as pltpu

N_HEADS = 8
HEAD_DIM = 64
V_DIM = 2 * HEAD_DIM
CONV_WIDTH = 31
CONV_HALF = CONV_WIDTH // 2
N_MOD = 6
GRID_W = 64
ROPE_THETA = 10000.0
CAPACITY_FACTOR = 2
EPS = 1e-6

LANES = 128
SUBLANES = 8
HALO = 16
VMEM_LIMIT = 56 * 1024 * 1024

F32 = jnp.float32
BF16 = jnp.bfloat16


def _cparams(sem):
    return pltpu.CompilerParams(dimension_semantics=sem, vmem_limit_bytes=VMEM_LIMIT)


def _tile(n, pref):
    t = min(n, pref)
    while n % t:
        t //= 2
    return t


def _split_bf16(a):
    hi = a.astype(BF16)
    lo = (a - hi.astype(F32)).astype(BF16)
    return hi, lo


def _dot3(a, b, dims):
    ah, al = _split_bf16(a)
    bh, bl = _split_bf16(b)
    dg = functools.partial(lax.dot_general, dimension_numbers=dims, preferred_element_type=F32)
    return dg(ah, bh) + (dg(ah, bl) + dg(al, bh))


_NN = (((1,), (0,)), ((), ()))
_NT = (((1,), (1,)), ((), ()))


def _sigmoid(x):
    return 1.0 / (1.0 + jnp.exp(-x))


def _adaln_kernel(c_ref, w_ref, b_ref, o_ref):
    c = c_ref[...]
    s = c * _sigmoid(c)
    o_ref[...] = _dot3(s, w_ref[...], _NN) + b_ref[...]


def _adaln(cc, w, b):
    rows, d = cc.shape
    n = w.shape[1]
    tn = _tile(n, 1024)
    return pl.pallas_call(
        _adaln_kernel,
        grid=(n // tn,),
        in_specs=[pl.BlockSpec((rows, d), lambda j: (0, 0)),
                  pl.BlockSpec((d, tn), lambda j: (0, j)),
                  pl.BlockSpec((1, tn), lambda j: (0, j))],
        out_specs=pl.BlockSpec((rows, tn), lambda j: (0, j)),
        out_shape=jax.ShapeDtypeStruct((rows, n), F32),
        compiler_params=_cparams(("arbitrary",)),
        name="adaln",
    )(cc, w, b)


def _norm_mod(x_ref, mod_ref, g_ref):
    x = x_ref[...]
    ms = jnp.mean(x * x, axis=-1, keepdims=True)
    y = x * lax.rsqrt(ms + EPS) * g_ref[...]
    return (y * (1.0 + mod_ref[0, 1:2, :]) + mod_ref[0, 0:1, :]).astype(BF16)


def _rope(acc, cos, sin_signed):
    lane = lax.broadcasted_iota(jnp.int32, (acc.shape[0], LANES), 1)
    first_half = (lane % 32) < 16
    outs = []
    for hh in range(acc.shape[1] // LANES):
        seg = acc[:, hh * LANES:(hh + 1) * LANES]
        partner = jnp.where(first_half, pltpu.roll(seg, LANES - 16, 1), pltpu.roll(seg, 16, 1))
        outs.append(seg * cos + partner * sin_signed)
    return jnp.concatenate(outs, axis=1)


def _proj_ctx_kernel(x_ref, mod_ref, g_ref, wk_ref, wv_ref, o_ref):
    h = _norm_mod(x_ref, mod_ref, g_ref)
    o_ref[0] = jnp.dot(h, wk_ref[...], preferred_element_type=F32).astype(o_ref.dtype)
    o_ref[1] = jnp.dot(h, wv_ref[...], preferred_element_type=F32).astype(o_ref.dtype)


def _proj_ctx(x2, mod3, g, w_bf, k_group):
    m, d = x2.shape
    tm = _tile(m, 512)
    return pl.pallas_call(
        _proj_ctx_kernel,
        grid=(m // tm,),
        in_specs=[pl.BlockSpec((tm, d), lambda i: (i, 0)),
                  pl.BlockSpec((1, N_MOD, d), lambda i: (0, 0, 0)),
                  pl.BlockSpec((1, d), lambda i: (0, 0)),
                  pl.BlockSpec((d, d), lambda i: (0, k_group)),
                  pl.BlockSpec((d, d), lambda i: (0, k_group + 1))],
        out_specs=pl.BlockSpec((2, tm, d), lambda i: (0, i, 0)),
        out_shape=jax.ShapeDtypeStruct((2, m, d), BF16),
        compiler_params=_cparams(("parallel",)),
        name="proj_ctx",
    )(x2, mod3, g, w_bf, w_bf)


def _proj_all_kernel(x_ref, mod_ref, g_ref, w_ref, cos_ref, sin_ref, u_ref, qkv_ref, gate_ref):
    d = x_ref.shape[1]
    h = _norm_mod(x_ref, mod_ref, g_ref)

    def group(n):
        return jnp.dot(h, w_ref[:, n * d:(n + 1) * d], preferred_element_type=F32)

    cos = cos_ref[...]
    sin = sin_ref[...]
    u_ref[...] = (group(0) * _sigmoid(group(1))).astype(u_ref.dtype)
    qkv_ref[0] = _rope(group(2) * (HEAD_DIM ** -0.5 * math.log2(math.e)), cos, sin).astype(qkv_ref.dtype)
    qkv_ref[1] = _rope(group(3), cos, sin).astype(qkv_ref.dtype)
    qkv_ref[2] = group(4).astype(qkv_ref.dtype)
    gate_ref[0] = _sigmoid(group(5)).astype(gate_ref.dtype)
    gate_ref[1] = _sigmoid(group(6)).astype(gate_ref.dtype)


def _proj_all(x2, mod3, g, w_bf, cos, sin_signed, t):
    m, d = x2.shape
    tm = _tile(t, 512)
    tblocks = t // tm
    return pl.pallas_call(
        _proj_all_kernel,
        grid=(m // tm,),
        in_specs=[pl.BlockSpec((tm, d), lambda i: (i, 0)),
                  pl.BlockSpec((1, N_MOD, d), lambda i: (i // tblocks, 0, 0)),
                  pl.BlockSpec((1, d), lambda i: (0, 0)),
                  pl.BlockSpec(w_bf.shape, lambda i: (0, 0), pipeline_mode=pl.Buffered(1)),
                  pl.BlockSpec((tm, LANES), lambda i: (i % tblocks, 0)),
                  pl.BlockSpec((tm, LANES), lambda i: (i % tblocks, 0))],
        out_specs=[pl.BlockSpec((tm, d), lambda i: (i, 0)),
                   pl.BlockSpec((3, tm, d), lambda i: (0, i, 0)),
                   pl.BlockSpec((2, tm, d), lambda i: (0, i, 0))],
        out_shape=[jax.ShapeDtypeStruct((m, d), BF16),
                   jax.ShapeDtypeStruct((3, m, d), BF16),
                   jax.ShapeDtypeStruct((2, m, d), BF16)],
        compiler_params=_cparams(("parallel",)),
        name="proj_all",
    )(x2, mod3, g, w_bf, cos, sin_signed)


def _conv_kernel(up_ref, um_ref, un_ref, gc_ref, wdw_ref, bdw_ref, lng_ref, lnb_ref, wc_ref, bc_ref,
                 o_ref, buf, cv, *, tt, rb):
    ti = pl.program_id(1)
    nt = pl.num_programs(1)
    d = um_ref.shape[2]
    prev = up_ref[0].astype(F32)
    nxt = un_ref[0].astype(F32)
    buf[0:HALO, :] = jnp.where(ti > 0, prev, 0.0)
    buf[HALO:HALO + tt, :] = um_ref[0].astype(F32)
    buf[HALO + tt:2 * HALO + tt, :] = jnp.where(ti < nt - 1, nxt, 0.0)

    off = HALO - CONV_HALF
    taps = [[(s - off, s // SUBLANES) for s in range(off, off + CONV_WIDTH) if s % SUBLANES == r]
            for r in range(SUBLANES)]
    n_slab = (off + CONV_WIDTH - 1) // SUBLANES + 1
    for c in range(d // LANES):
        cs = slice(c * LANES, (c + 1) * LANES)
        for rblk in range(tt // rb):
            r0 = rblk * rb
            slabs = [buf[r0 + SUBLANES * a:r0 + SUBLANES * a + rb + SUBLANES, cs] for a in range(n_slab)]
            acc = None
            for r in range(SUBLANES):
                z = None
                for k, a in taps[r]:
                    term = slabs[a] * wdw_ref[k:k + 1, cs]
                    z = term if z is None else z + term
                z = z[r:r + rb]
                acc = z if acc is None else acc + z
            cv[r0:r0 + rb, cs] = acc

    v = cv[...] + bdw_ref[...]
    mu = jnp.mean(v, axis=-1, keepdims=True)
    xc = v - mu
    var = jnp.mean(xc * xc, axis=-1, keepdims=True)
    y = xc * lax.rsqrt(var + EPS) * lng_ref[...] + lnb_ref[...]
    z = y * _sigmoid(y)
    yc = jnp.dot(z.astype(BF16), wc_ref[...], preferred_element_type=F32) + bc_ref[...]
    o_ref[...] = (gc_ref[0].astype(F32) * yc).astype(o_ref.dtype)


def _conv_branch(u3, gates, wdw, bdw, lng, lnb, wc_bf, bc):
    bsz, t, dc = u3.shape
    d = wc_bf.shape[1]
    tt = _tile(t, 256)
    nt = t // tt
    hb = tt // HALO
    nhb = t // HALO
    kern = functools.partial(_conv_kernel, tt=tt, rb=_tile(tt, 64))
    const = lambda shape: pl.BlockSpec(shape, lambda b, i: (0,) * len(shape))
    return pl.pallas_call(
        kern,
        grid=(bsz, nt),
        in_specs=[pl.BlockSpec((1, HALO, dc), lambda b, i: (b, jnp.maximum(i * hb - 1, 0), 0)),
                  pl.BlockSpec((1, tt, dc), lambda b, i: (b, i, 0)),
                  pl.BlockSpec((1, HALO, dc), lambda b, i: (b, jnp.minimum((i + 1) * hb, nhb - 1), 0)),
                  pl.BlockSpec((1, tt, d), lambda b, i: (0, b * nt + i, 0)),
                  const((CONV_WIDTH, dc)), const((1, dc)), const((1, dc)), const((1, dc)),
                  const((dc, d)), const((1, d))],
        out_specs=pl.BlockSpec((tt, d), lambda b, i: (b * nt + i, 0)),
        out_shape=jax.ShapeDtypeStruct((bsz * t, d), BF16),
        scratch_shapes=[pltpu.VMEM((tt + 2 * HALO, dc), F32), pltpu.VMEM((tt, dc), F32)],
        compiler_params=_cparams(("parallel", "parallel")),
        name="conv",
    )(u3, u3, u3, gates, wdw, bdw, lng, lnb, wc_bf, bc)


def _attn_kernel(lam_ref, q_ref, kl_ref, vl_ref, kc_ref, vc_ref, gs_ref, o_ref, vext, s_scr, m_scr,
                 *, tq, kb, nq, n_tiles, out_scale):
    n = pl.program_id(0)
    t = kl_ref.shape[1]
    pair = lax.div(n, nq)
    prev_pair = lax.div(jnp.maximum(n - 1, 0), nq)

    @pl.when((lax.rem(n, nq) == 0) & (n < n_tiles))
    def _():
        slot = lax.rem(pair, 2)
        vext[slot, 0:V_DIM, 0:t] = vl_ref[0].astype(F32).T.astype(vext.dtype)
        vext[slot, 0:V_DIM, t:] = vc_ref[0].astype(F32).T.astype(vext.dtype)
        row = lax.broadcasted_iota(jnp.int32, (vext.shape[1] - V_DIM, vext.shape[2]), 0)
        vext[slot, V_DIM:, :] = jnp.where(row == 0, 1.0, 0.0).astype(vext.dtype)

    @pl.when(n == 0)
    def _():
        s_scr[...] = jnp.zeros_like(s_scr)
        m_scr[...] = jnp.zeros_like(m_scr)

    q = q_ref[0]
    lane = lax.broadcasted_iota(jnp.int32, q.shape, 1)
    zero = jnp.zeros_like(q)
    qs = jnp.concatenate([jnp.where(lane < HEAD_DIM, q, zero), jnp.where(lane >= HEAD_DIM, q, zero)], axis=0)
    qs_t = qs.astype(F32).T.astype(BF16)
    vprev = vext.at[lax.rem(prev_pair, 2)]
    m_old = m_scr[...]
    acc = None
    m_new = None
    blocks = [(kl_ref, r, kb) for r in range(0, t, kb)] + [(kc_ref, 0, kc_ref.shape[1])]
    row = 0
    for k_ref, r0, size in blocks:
        rows = slice(row, row + size)
        p = jnp.exp2(s_scr[rows, :] - m_old).astype(BF16)
        part = jnp.dot(vprev[:, rows], p, preferred_element_type=F32)
        acc = part if acc is None else acc + part
        s_blk = jnp.dot(k_ref[0, r0:r0 + size, :], qs_t, preferred_element_type=F32)
        s_scr[rows, :] = s_blk
        blk_max = jnp.max(s_blk, axis=0, keepdims=True)
        m_new = blk_max if m_new is None else jnp.maximum(m_new, blk_max)
        row += size
    m_scr[...] = m_new

    o = acc[:V_DIM] / acc[V_DIM:V_DIM + 1]
    o = o[:, :tq] - lam_ref[0] * o[:, tq:]
    ms = jnp.mean(o * o, axis=0, keepdims=True)
    o = o * lax.rsqrt(ms + EPS) * gs_ref[...] * out_scale
    o_ref[...] = o.T.astype(o_ref.dtype)


def _attention(lam, qkv, kvc, g_subln, bsz, t, n_ctx, out_scale):
    m, width = qkv.shape[1], qkv.shape[2]
    nh = width // V_DIM
    tq = _tile(t, 512)
    nq = t // tq
    s_all = t + n_ctx
    n_tiles = bsz * nh * nq
    kern = functools.partial(_attn_kernel, tq=tq, kb=_tile(t, 256), nq=nq, n_tiles=n_tiles, out_scale=out_scale)

    def tile_coords(n):
        pair = lax.div(n, nq)
        return lax.div(pair, nh), lax.rem(pair, nh), lax.rem(n, nq)

    def started(n):
        return tile_coords(jnp.minimum(n, n_tiles - 1))

    def finished(n):
        return tile_coords(jnp.maximum(n - 1, 0))

    def q_map(n):
        b, h, j = started(n)
        return 0, b * nq + j, h

    def kv_map(which):
        def index(n):
            b, h, _ = started(n)
            return which, b, h
        return index

    def out_map(n):
        b, h, j = finished(n)
        return b * nq + j, h

    return pl.pallas_call(
        kern,
        grid=(n_tiles + 1,),
        in_specs=[pl.BlockSpec(memory_space=pltpu.SMEM),
                  pl.BlockSpec((1, tq, V_DIM), q_map),
                  pl.BlockSpec((1, t, V_DIM), kv_map(1)),
                  pl.BlockSpec((1, t, V_DIM), kv_map(2)),
                  pl.BlockSpec((1, n_ctx, V_DIM), kv_map(0)),
                  pl.BlockSpec((1, n_ctx, V_DIM), kv_map(1)),
                  pl.BlockSpec((V_DIM, 1), lambda n: (0, 0))],
        out_specs=pl.BlockSpec((tq, V_DIM), out_map),
        out_shape=jax.ShapeDtypeStruct((m, width), BF16),
        scratch_shapes=[pltpu.VMEM((2, V_DIM + 2 * SUBLANES, s_all), BF16),
                        pltpu.VMEM((s_all, 2 * tq), F32),
                        pltpu.VMEM((1, 2 * tq), F32)],
        compiler_params=_cparams(("arbitrary",)),
        name="attn",
    )(lam, qkv, qkv, qkv, kvc, kvc, g_subln.reshape(V_DIM, 1))


def _post_kernel(o_ref, ga_ref, gy_ref, x_ref, mod_ref, wa_ref, wo_ref, gf_ref, wrt_ref,
                 xmid_ref, hf_ref, aff_ref):
    y_attn = jnp.dot(o_ref[...], wa_ref[...], preferred_element_type=F32)
    merged = ga_ref[0].astype(F32) * y_attn + gy_ref[...].astype(F32)
    mix = jnp.dot(merged.astype(BF16), wo_ref[...], preferred_element_type=F32)
    xm = x_ref[...] + mod_ref[0, 2:3, :] * mix
    xmid_ref[...] = xm
    ms = jnp.mean(xm * xm, axis=-1, keepdims=True)
    hf = xm * lax.rsqrt(ms + EPS) * gf_ref[...]
    hf = hf * (1.0 + mod_ref[0, 4:5, :]) + mod_ref[0, 3:4, :]
    hf_ref[...] = hf.astype(hf_ref.dtype)
    logits = _dot3(wrt_ref[...], hf, _NT)
    z = jnp.exp(logits - jnp.max(logits, axis=0, keepdims=True))
    aff_ref[0] = z / jnp.sum(z, axis=0, keepdims=True)


def _post(o, gates, gy, x2, mod3, wa_bf, wo_bf, gf, wrt, bsz, t):
    m, d = x2.shape
    aw = o.shape[1]
    ne = wrt.shape[0]
    tm = _tile(t, 512)
    nt = t // tm
    const = lambda shape: pl.BlockSpec(shape, lambda i: (0,) * len(shape))
    return pl.pallas_call(
        _post_kernel,
        grid=(m // tm,),
        in_specs=[pl.BlockSpec((tm, aw), lambda i: (i, 0)),
                  pl.BlockSpec((1, tm, d), lambda i: (1, i, 0)),
                  pl.BlockSpec((tm, d), lambda i: (i, 0)),
                  pl.BlockSpec((tm, d), lambda i: (i, 0)),
                  pl.BlockSpec((1, N_MOD, d), lambda i: (i // nt, 0, 0)),
                  const((aw, d)), const((d, d)), const((1, d)), const((ne, d))],
        out_specs=[pl.BlockSpec((tm, d), lambda i: (i, 0)),
                   pl.BlockSpec((tm, d), lambda i: (i, 0)),
                   pl.BlockSpec((1, ne, tm), lambda i: (i // nt, 0, i % nt))],
        out_shape=[jax.ShapeDtypeStruct((m, d), F32),
                   jax.ShapeDtypeStruct((m, d), BF16),
                   jax.ShapeDtypeStruct((bsz, ne, t), F32)],
        compiler_params=_cparams(("parallel",)),
        name="post",
    )(o, gates, gy, x2, mod3, wa_bf, wo_bf, gf, wrt)


def _count(mask):
    ones = jnp.where(mask, 1.0, 0.0)
    return jnp.sum(jnp.sum(ones, axis=2, keepdims=True), axis=1, keepdims=True)


def _route_kernel(a_ref, pos_ref, rank_ref, *, cap):
    a = a_ref[0]
    ne, nc, ln = a.shape
    rows = ne * nc
    bits = lax.bitcast_convert_type(a, jnp.int32)

    def search(i, cur):
        cand = cur | jnp.left_shift(jnp.int32(1), 30 - i)
        return jnp.where(_count(bits >= cand) >= cap, cand, cur)

    tau = lax.fori_loop(0, 31, search, jnp.zeros((ne, 1, 1), jnp.int32))

    r_i = lax.broadcasted_iota(jnp.int32, (rows, rows), 0)
    r_j = lax.broadcasted_iota(jnp.int32, (rows, rows), 1)
    shift = int(math.log2(nc))
    same_expert = lax.shift_right_logical(r_i, shift) == lax.shift_right_logical(r_j, shift)
    chunk_before = jnp.where(same_expert & (r_j < r_i), 1.0, 0.0).astype(BF16)
    l_i = lax.broadcasted_iota(jnp.int32, (ln, ln), 0)
    l_j = lax.broadcasted_iota(jnp.int32, (ln, ln), 1)
    lane_upto = jnp.where(l_i <= l_j, 1.0, 0.0).astype(BF16)

    def excl_prefix(mask):
        xm = jnp.where(mask, 1.0, 0.0).reshape(rows, ln)
        incl = jnp.dot(xm.astype(BF16), lane_upto, preferred_element_type=F32)
        tot = jnp.broadcast_to(incl[:, ln - 1:ln], (rows, ln))
        offs = jnp.dot(chunk_before, tot.astype(BF16), preferred_element_type=F32)
        return (incl + offs - xm).reshape(ne, nc, ln)

    gt = bits > tau
    eq = bits == tau
    need = cap - _count(gt)
    sel = gt | (eq & (excl_prefix(eq) < need))
    pos = excl_prefix(sel)
    pos_ref[0] = jnp.where(sel, pos, -1.0).astype(jnp.int32)
    rank_ref[0] = pos.astype(jnp.int32)


def _route(aff4, cap):
    bsz, ne, nc, ln = aff4.shape
    assert nc & (nc - 1) == 0, "token chunks per sample must be a power of two"
    spec = pl.BlockSpec((1, ne, nc, ln), lambda b: (b, 0, 0, 0))
    return pl.pallas_call(
        functools.partial(_route_kernel, cap=cap),
        grid=(bsz,),
        in_specs=[spec],
        out_specs=[spec, spec],
        out_shape=[jax.ShapeDtypeStruct((bsz, ne, nc, ln), jnp.int32)] * 2,
        compiler_params=_cparams(("parallel",)),
        name="route",
    )(aff4)


def _slot_range(cs_ref, b, e, ne, nchunk, c):
    base = (b * ne + e) * (nchunk + 1) + c
    lo, hi = cs_ref[base], cs_ref[base + 1]
    return lax.shift_right_logical(lo, 4) * 16, hi


def _max_windows(starts_and_ends, win):
    n = None
    for start, end in starts_and_ends:
        n_e = lax.div(end - start + (win - 1), win)
        n = n_e if n is None else jnp.maximum(n, n_e)
    return n


def _gather_kernel(cs_ref, pos_ref, aff_ref, hf_ref, xs_ref, gate_ref, acc, gacc, *, cap, tc, win):
    b, eg = pl.program_id(0), pl.program_id(1)
    group = pos_ref.shape[1]
    ne = pl.num_programs(1) * group
    nchunk = hf_ref.shape[1] // tc
    acc[...] = jnp.zeros_like(acc)
    gacc[...] = jnp.zeros_like(gacc)
    for c in range(nchunk):
        ranges = [_slot_range(cs_ref, b, eg * group + g, ne, nchunk, c) for g in range(group)]
        cols = slice(c * tc, (c + 1) * tc)

        def window(w, carry):
            rows = [pl.multiple_of(jnp.minimum(r0 + w * win, cap), 16) for r0, _ in ranges]
            hits = [pos_ref[0, g, :, cols] == r + lax.broadcasted_iota(jnp.int32, (win, tc), 0)
                    for g, r in enumerate(rows)]
            onehot = jnp.concatenate([jnp.where(h, 1.0, 0.0).astype(BF16) for h in hits], axis=0)
            part = jnp.dot(onehot, hf_ref[0, cols, :], preferred_element_type=F32)
            for g, r in enumerate(rows):
                acc[g, pl.ds(r, win), :] += part[g * win:(g + 1) * win]
                gacc[g, pl.ds(r, win), :] += jnp.sum(jnp.where(hits[g], aff_ref[0, g, :, cols], 0.0),
                                                     axis=1, keepdims=True)
            return carry

        lax.fori_loop(0, _max_windows(ranges, win), window, 0)
    for g in range(group):
        xs_ref[g, 0] = acc[g, 0:cap, :].astype(xs_ref.dtype)
        gate_ref[g, 0] = gacc[g, 0:cap, :]


def _gather(chunk_slots, pos4, aff4, hf3, cap, tc):
    bsz, ne, _, t = pos4.shape
    d = hf3.shape[2]
    win = _tile(cap, 64)
    group = _tile(ne, 8)
    kern = functools.partial(_gather_kernel, cap=cap, tc=tc, win=win)
    return pl.pallas_call(
        kern,
        grid_spec=pltpu.PrefetchScalarGridSpec(
            num_scalar_prefetch=1,
            grid=(bsz, ne // group),
            in_specs=[pl.BlockSpec((1, group, 1, t), lambda b, e, cs: (b, e, 0, 0)),
                      pl.BlockSpec((1, group, 1, t), lambda b, e, cs: (b, e, 0, 0)),
                      pl.BlockSpec((1, t, d), lambda b, e, cs: (b, 0, 0), pipeline_mode=pl.Buffered(1))],
            out_specs=[pl.BlockSpec((group, 1, cap, d), lambda b, e, cs: (e, b, 0, 0)),
                       pl.BlockSpec((group, 1, cap, 1), lambda b, e, cs: (e, b, 0, 0))],
            scratch_shapes=[pltpu.VMEM((group, cap + win, d), F32), pltpu.VMEM((group, cap + win, 1), F32)]),
        out_shape=[jax.ShapeDtypeStruct((ne, bsz, cap, d), BF16),
                   jax.ShapeDtypeStruct((ne, bsz, cap, 1), F32)],
        compiler_params=_cparams(("parallel", "arbitrary")),
        name="gather",
    )(chunk_slots, pos4, aff4, hf3)


def _expert_kernel(xs_ref, gate_ref, wg_ref, wu_ref, wd_ref, o_ref, acc):
    f = pl.program_id(2)

    @pl.when(f == 0)
    def _():
        acc[...] = jnp.zeros_like(acc)

    x = xs_ref[0]
    g = jnp.dot(x, wg_ref[0].astype(BF16), preferred_element_type=F32)
    u = jnp.dot(x, wu_ref[0].astype(BF16), preferred_element_type=F32)
    hid = (g * _sigmoid(g)) * u
    acc[...] += jnp.dot(hid.astype(BF16), wd_ref[0].astype(BF16), preferred_element_type=F32)

    @pl.when(f == pl.num_programs(2) - 1)
    def _():
        o_ref[0] = (acc[...] * gate_ref[0]).astype(o_ref.dtype)


def _experts(xs3, gate3, wg, wu, wd):
    ne, r, d = xs3.shape
    ff = wg.shape[2]
    tr = _tile(r, 2048)
    tf = _tile(ff, 256)
    return pl.pallas_call(
        _expert_kernel,
        grid=(ne, r // tr, ff // tf),
        in_specs=[pl.BlockSpec((1, tr, d), lambda e, i, f: (e, i, 0)),
                  pl.BlockSpec((1, tr, 1), lambda e, i, f: (e, i, 0)),
                  pl.BlockSpec((1, d, tf), lambda e, i, f: (e, 0, f)),
                  pl.BlockSpec((1, d, tf), lambda e, i, f: (e, 0, f)),
                  pl.BlockSpec((1, tf, d), lambda e, i, f: (e, f, 0))],
        out_specs=pl.BlockSpec((1, tr, d), lambda e, i, f: (e, i, 0)),
        out_shape=jax.ShapeDtypeStruct((ne, r, d), BF16),
        scratch_shapes=[pltpu.VMEM((tr, d), F32)],
        compiler_params=_cparams(("parallel", "parallel", "arbitrary")),
        name="experts",
    )(xs3, gate3, wg, wu, wd)


def _combine_kernel(cs_ref, pos_ref, ys_ref, xm_ref, mod_ref, gfin_ref, o_ref, acc, pos_t, *, cap, win, group):
    b, i = pl.program_id(0), pl.program_id(1)
    nchunk = pl.num_programs(1)
    ne, tt = pos_ref.shape[1], pos_ref.shape[2]
    p = pos_ref[0].astype(F32)
    padded = jnp.concatenate([p, jnp.full((LANES - ne, tt), -1.0, F32)], axis=0)
    pos_t[...] = padded.T
    acc[...] = jnp.zeros_like(acc)
    for e0 in range(0, ne, group):
        experts = range(e0, e0 + group)
        ranges = [_slot_range(cs_ref, b, e, ne, nchunk, i) for e in experts]

        def window(w, carry):
            onehots, outputs = [], []
            for e, (k0, _) in zip(experts, ranges):
                first = k0 + w * win
                k = pl.multiple_of(jnp.minimum(first, cap - win), 16)
                slot = k + lax.broadcasted_iota(jnp.int32, (tt, win), 1)
                mine = (pos_t[:, e:e + 1] == slot.astype(F32)) & (slot >= first)
                onehots.append(jnp.where(mine, 1.0, 0.0).astype(BF16))
                outputs.append(ys_ref[e, 0, pl.ds(k, win), :])
            acc[...] += jnp.dot(jnp.concatenate(onehots, axis=1), jnp.concatenate(outputs, axis=0),
                                preferred_element_type=F32)
            return carry

        lax.fori_loop(0, _max_windows(ranges, win), window, 0)
    xo = xm_ref[...] + mod_ref[0, 5:6, :] * acc[...]
    ms = jnp.mean(xo * xo, axis=-1, keepdims=True)
    o_ref[...] = xo * lax.rsqrt(ms + EPS) * gfin_ref[...]


def _combine(chunk_slots, pos3, ys4, xmid, mod3, gfin, cap, tc):
    bsz, ne, t = pos3.shape
    m, d = xmid.shape
    nt = t // tc
    return pl.pallas_call(
        functools.partial(_combine_kernel, cap=cap, win=_tile(cap, 64), group=_tile(ne, 8)),
        grid_spec=pltpu.PrefetchScalarGridSpec(
            num_scalar_prefetch=1,
            grid=(bsz, nt),
            in_specs=[pl.BlockSpec((1, ne, tc), lambda b, i, cs: (b, 0, i)),
                      pl.BlockSpec((ne, 1, cap, d), lambda b, i, cs: (0, b, 0, 0), pipeline_mode=pl.Buffered(1)),
                      pl.BlockSpec((tc, d), lambda b, i, cs: (b * nt + i, 0)),
                      pl.BlockSpec((1, N_MOD, d), lambda b, i, cs: (b, 0, 0)),
                      pl.BlockSpec((1, d), lambda b, i, cs: (0, 0))],
            out_specs=pl.BlockSpec((tc, d), lambda b, i, cs: (b * nt + i, 0)),
            scratch_shapes=[pltpu.VMEM((tc, d), F32), pltpu.VMEM((tc, LANES), F32)]),
        out_shape=jax.ShapeDtypeStruct((m, d), F32),
        compiler_params=_cparams(("parallel", "arbitrary")),
        name="combine",
    )(chunk_slots, pos3, ys4, xmid, mod3, gfin)


def _rope_tables(t):
    n_freq = HEAD_DIM // 4
    tok = jnp.arange(t)
    inv = ROPE_THETA ** (-jnp.arange(n_freq, dtype=F32) / n_freq)
    ang_r = (tok // GRID_W)[:, None].astype(F32) * inv
    ang_c = (tok % GRID_W)[:, None].astype(F32) * inv
    cos64 = jnp.concatenate([jnp.cos(ang_r)] * 2 + [jnp.cos(ang_c)] * 2, axis=-1)
    sin64 = jnp.concatenate([-jnp.sin(ang_r), jnp.sin(ang_r), -jnp.sin(ang_c), jnp.sin(ang_c)], axis=-1)
    return jnp.tile(cos64, (1, 2)), jnp.tile(sin64, (1, 2))


def kernel(x, c, ctx, c_ctx, w_ada, b_ada, g_norm_mix, g_norm_ffn, w_in, w_dw, b_dw, ln_g_conv, ln_b_conv,
           w_conv_out, b_conv_out, lambda_q1, lambda_k1, lambda_q2, lambda_k2, g_subln, w_attn_out, w_out,
           w_router, w_expert_gate, w_expert_up, w_expert_down, g_final):
    bsz, t, d = x.shape
    n_ctx = ctx.shape[1]
    depth = w_ada.shape[0]
    assert depth == 1, "single-layer trunk: the context stream only feeds keys/values"
    ne = w_router.shape[2]
    cap = CAPACITY_FACTOR * t // ne
    dc = w_dw.shape[2]
    qkw = N_HEADS * 2 * HEAD_DIM
    aw = N_HEADS * V_DIM
    assert dc == d and qkw == d and aw == d, "projection groups are addressed as equal-width column blocks"
    layer = 0
    lam_init = 0.8 - 0.6 * math.exp(-0.3 * layer)
    lam = (jnp.exp(jnp.sum(lambda_q1[layer].astype(F32) * lambda_k1[layer].astype(F32)))
           - jnp.exp(jnp.sum(lambda_q2[layer].astype(F32) * lambda_k2[layer].astype(F32)))
           + lam_init).reshape(1)

    row = lambda v: v.reshape(1, -1)
    x2 = x.reshape(bsz * t, d)
    ctx2 = ctx.reshape(bsz * n_ctx, d)

    pad = (-(bsz + 1)) % 8
    cc = jnp.concatenate([c, c_ctx[None, :], jnp.zeros((pad, d), F32)], axis=0)
    mod = _adaln(cc, w_ada[layer], row(b_ada[layer]))
    mod_lat = mod[:bsz].reshape(bsz, N_MOD, d)
    mod_ctx = mod[bsz:bsz + 1].reshape(1, N_MOD, d)

    w_in_bf = w_in[layer].astype(BF16)
    g_mix = row(g_norm_mix[layer])
    cos, sin_signed = _rope_tables(t)

    u, qkv, gates = _proj_all(x2, mod_lat, g_mix, w_in_bf, cos, sin_signed, t)
    kvc = _proj_ctx(ctx2, mod_ctx, g_mix, w_in_bf, 3)

    gy = _conv_branch(u.reshape(bsz, t, dc), gates, w_dw[layer], row(b_dw[layer]), row(ln_g_conv[layer]),
                      row(ln_b_conv[layer]), w_conv_out[layer].astype(BF16), row(b_conv_out[layer]))
    o = _attention(lam, qkv, kvc, row(g_subln[layer]), bsz, t, n_ctx, 1.0 - lam_init)

    xmid, hf, aff_t = _post(o, gates, gy, x2, mod_lat, w_attn_out[layer].astype(BF16), w_out[layer].astype(BF16),
                            row(g_norm_ffn[layer]), w_router[layer].T, bsz, t)

    pos, rank = _route(aff_t.reshape(bsz, ne, t // LANES, LANES), cap)
    tc = _tile(t, 2 * LANES)
    chunk_slots = jnp.concatenate([rank[:, :, ::tc // LANES, 0], jnp.full((bsz, ne, 1), cap, jnp.int32)],
                                  axis=-1).reshape(-1)
    xs, gate = _gather(chunk_slots, pos.reshape(bsz, ne, 1, t), aff_t.reshape(bsz, ne, 1, t),
                       hf.reshape(bsz, t, d), cap, tc)
    ys = _experts(xs.reshape(ne, bsz * cap, d), gate.reshape(ne, bsz * cap, 1),
                  w_expert_gate[layer], w_expert_up[layer], w_expert_down[layer])
    out = _combine(chunk_slots, pos.reshape(bsz, ne, t), ys.reshape(ne, bsz, cap, d), xmid, mod_lat,
                   row(g_final), cap, tc)
    return out.reshape(bsz, t, d)
```

```python
import functools
import math

import jax
import jax.numpy as jnp
from jax import lax
from jax.experimental import pallas as pl
from jax.experimental.pallas import tpu as pltpu

N_HEADS = 8
HEAD_DIM = 64
V_DIM = 2 * HEAD_DIM
CONV_WIDTH = 31
CONV_HALF = CONV_WIDTH // 2
N_MOD = 6
GRID_W = 64
ROPE_THETA = 10000.0
CAPACITY_FACTOR = 2
EPS = 1e-6

LANES = 128
SUBLANES = 8
HALO = 16
VMEM_LIMIT = 56 * 1024 * 1024

F32 = jnp.float32
BF16 = jnp.bfloat16


def _cparams(sem):
    return pltpu.CompilerParams(dimension_semantics=sem, vmem_limit_bytes=VMEM_LIMIT)


def _tile(n, pref):
    t = min(n, pref)
    while n % t:
        t //= 2
    return t


def _split_bf16(a):
    hi = a.astype(BF16)
    lo = (a - hi.astype(F32)).astype(BF16)
    return hi, lo


def _dot3(a, b, dims):
    ah, al = _split_bf16(a)
    bh, bl = _split_bf16(b)
    dg = functools.partial(lax.dot_general, dimension_numbers=dims, preferred_element_type=F32)
    return dg(ah, bh) + (dg(ah, bl) + dg(al, bh))


_NN = (((1,), (0,)), ((), ()))
_NT = (((1,), (1,)), ((), ()))


def _sigmoid(x):
    return 1.0 / (1.0 + jnp.exp(-x))


def _adaln_kernel(c_ref, w_ref, b_ref, o_ref):
    c = c_ref[...]
    s = c * _sigmoid(c)
    o_ref[...] = _dot3(s, w_ref[...], _NN) + b_ref[...]


def _adaln(cc, w, b):
    rows, d = cc.shape
    n = w.shape[1]
    tn = _tile(n, 1024)
    return pl.pallas_call(
        _adaln_kernel,
        grid=(n // tn,),
        in_specs=[pl.BlockSpec((rows, d), lambda j: (0, 0)),
                  pl.BlockSpec((d, tn), lambda j: (0, j)),
                  pl.BlockSpec((1, tn), lambda j: (0, j))],
        out_specs=pl.BlockSpec((rows, tn), lambda j: (0, j)),
        out_shape=jax.ShapeDtypeStruct((rows, n), F32),
        compiler_params=_cparams(("arbitrary",)),
        name="adaln",
    )(cc, w, b)


def _norm_mod(x_ref, mod_ref, g_ref):
    x = x_ref[...]
    ms = jnp.mean(x * x, axis=-1, keepdims=True)
    y = x * lax.rsqrt(ms + EPS) * g_ref[...]
    return (y * (1.0 + mod_ref[0, 1:2, :]) + mod_ref[0, 0:1, :]).astype(BF16)


def _rope(acc, cos, sin_signed):
    lane = lax.broadcasted_iota(jnp.int32, (acc.shape[0], LANES), 1)
    first_half = (lane % 32) < 16
    outs = []
    for hh in range(acc.shape[1] // LANES):
        seg = acc[:, hh * LANES:(hh + 1) * LANES]
        partner = jnp.where(first_half, pltpu.roll(seg, LANES - 16, 1), pltpu.roll(seg, 16, 1))
        outs.append(seg * cos + partner * sin_signed)
    return jnp.concatenate(outs, axis=1)


def _proj_ctx_kernel(x_ref, mod_ref, g_ref, wk_ref, wv_ref, o_ref):
    h = _norm_mod(x_ref, mod_ref, g_ref)
    o_ref[0] = jnp.dot(h, wk_ref[...], preferred_element_type=F32).astype(o_ref.dtype)
    o_ref[1] = jnp.dot(h, wv_ref[...], preferred_element_type=F32).astype(o_ref.dtype)


def _proj_ctx(x2, mod3, g, w_bf, k_group):
    m, d = x2.shape
    tm = _tile(m, 512)
    return pl.pallas_call(
        _proj_ctx_kernel,
        grid=(m // tm,),
        in_specs=[pl.BlockSpec((tm, d), lambda i: (i, 0)),
                  pl.BlockSpec((1, N_MOD, d), lambda i: (0, 0, 0)),
                  pl.BlockSpec((1, d), lambda i: (0, 0)),
                  pl.BlockSpec((d, d), lambda i: (0, k_group)),
                  pl.BlockSpec((d, d), lambda i: (0, k_group + 1))],
        out_specs=pl.BlockSpec((2, tm, d), lambda i: (0, i, 0)),
        out_shape=jax.ShapeDtypeStruct((2, m, d), BF16),
        compiler_params=_cparams(("parallel",)),
        name="proj_ctx",
    )(x2, mod3, g, w_bf, w_bf)


def _proj_all_kernel(x_ref, mod_ref, g_ref, w_ref, cos_ref, sin_ref, u_ref, qkv_ref, gate_ref):
    d = x_ref.shape[1]
    h = _norm_mod(x_ref, mod_ref, g_ref)

    def group(n):
        return jnp.dot(h, w_ref[:, n * d:(n + 1) * d], preferred_element_type=F32)

    cos = cos_ref[...]
    sin = sin_ref[...]
    u_ref[...] = (group(0) * _sigmoid(group(1))).astype(u_ref.dtype)
    qkv_ref[0] = _rope(group(2) * (HEAD_DIM ** -0.5 * math.log2(math.e)), cos, sin).astype(qkv_ref.dtype)
    qkv_ref[1] = _rope(group(3), cos, sin).astype(qkv_ref.dtype)
    qkv_ref[2] = group(4).astype(qkv_ref.dtype)
    gate_ref[0] = _sigmoid(group(5)).astype(gate_ref.dtype)
    gate_ref[1] = _sigmoid(group(6)).astype(gate_ref.dtype)


def _proj_all(x2, mod3, g, w_bf, cos, sin_signed, t):
    m, d = x2.shape
    tm = _tile(t, 512)
    tblocks = t // tm
    return pl.pallas_call(
        _proj_all_kernel,
        grid=(m // tm,),
        in_specs=[pl.BlockSpec((tm, d), lambda i: (i, 0)),
                  pl.BlockSpec((1, N_MOD, d), lambda i: (i // tblocks, 0, 0)),
                  pl.BlockSpec((1, d), lambda i: (0, 0)),
                  pl.BlockSpec(w_bf.shape, lambda i: (0, 0), pipeline_mode=pl.Buffered(1)),
                  pl.BlockSpec((tm, LANES), lambda i: (i % tblocks, 0)),
                  pl.BlockSpec((tm, LANES), lambda i: (i % tblocks, 0))],
        out_specs=[pl.BlockSpec((tm, d), lambda i: (i, 0)),
                   pl.BlockSpec((3, tm, d), lambda i: (0, i, 0)),
                   pl.BlockSpec((2, tm, d), lambda i: (0, i, 0))],
        out_shape=[jax.ShapeDtypeStruct((m, d), BF16),
                   jax.ShapeDtypeStruct((3, m, d), BF16),
                   jax.ShapeDtypeStruct((2, m, d), BF16)],
        compiler_params=_cparams(("parallel",)),
        name="proj_all",
    )(x2, mod3, g, w_bf, cos, sin_signed)


def _conv_kernel(up_ref, um_ref, un_ref, gc_ref, wdw_ref, bdw_ref, lng_ref, lnb_ref, wc_ref, bc_ref,
                 o_ref, buf, cv, *, tt, rb):
    ti = pl.program_id(1)
    nt = pl.num_programs(1)
    d = um_ref.shape[2]
    prev = up_ref[0].astype(F32)
    nxt = un_ref[0].astype(F32)
    buf[0:HALO, :] = jnp.where(ti > 0, prev, 0.0)
    buf[HALO:HALO + tt, :] = um_ref[0].astype(F32)
    buf[HALO + tt:2 * HALO + tt, :] = jnp.where(ti < nt - 1, nxt, 0.0)

    off = HALO - CONV_HALF
    taps = [[(s - off, s // SUBLANES) for s in range(off, off + CONV_WIDTH) if s % SUBLANES == r]
            for r in range(SUBLANES)]
    n_slab = (off + CONV_WIDTH - 1) // SUBLANES + 1
    for c in range(d // LANES):
        cs = slice(c * LANES, (c + 1) * LANES)
        for rblk in range(tt // rb):
            r0 = rblk * rb
            slabs = [buf[r0 + SUBLANES * a:r0 + SUBLANES * a + rb + SUBLANES, cs] for a in range(n_slab)]
            acc = None
            for r in range(SUBLANES):
                z = None
                for k, a in taps[r]:
                    term = slabs[a] * wdw_ref[k:k + 1, cs]
                    z = term if z is None else z + term
                z = z[r:r + rb]
                acc = z if acc is None else acc + z
            cv[r0:r0 + rb, cs] = acc

    v = cv[...] + bdw_ref[...]
    mu = jnp.mean(v, axis=-1, keepdims=True)
    xc = v - mu
    var = jnp.mean(xc * xc, axis=-1, keepdims=True)
    y = xc * lax.rsqrt(var + EPS) * lng_ref[...] + lnb_ref[...]
    z = y * _sigmoid(y)
    yc = jnp.dot(z.astype(BF16), wc_ref[...], preferred_element_type=F32) + bc_ref[...]
    o_ref[...] = (gc_ref[0].astype(F32) * yc).astype(o_ref.dtype)


def _conv_branch(u3, gates, wdw, bdw, lng, lnb, wc_bf, bc):
    bsz, t, dc = u3.shape
    d = wc_bf.shape[1]
    tt = _tile(t, 256)
    nt = t // tt
    hb = tt // HALO
    nhb = t // HALO
    kern = functools.partial(_conv_kernel, tt=tt, rb=_tile(tt, 64))
    const = lambda shape: pl.BlockSpec(shape, lambda b, i: (0,) * len(shape))
    return pl.pallas_call(
        kern,
        grid=(bsz, nt),
        in_specs=[pl.BlockSpec((1, HALO, dc), lambda b, i: (b, jnp.maximum(i * hb - 1, 0), 0)),
                  pl.BlockSpec((1, tt, dc), lambda b, i: (b, i, 0)),
                  pl.BlockSpec((1, HALO, dc), lambda b, i: (b, jnp.minimum((i + 1) * hb, nhb - 1), 0)),
                  pl.BlockSpec((1, tt, d), lambda b, i: (0, b * nt + i, 0)),
                  const((CONV_WIDTH, dc)), const((1, dc)), const((1, dc)), const((1, dc)),
                  const((dc, d)), const((1, d))],
        out_specs=pl.BlockSpec((tt, d), lambda b, i: (b * nt + i, 0)),
        out_shape=jax.ShapeDtypeStruct((bsz * t, d), BF16),
        scratch_shapes=[pltpu.VMEM((tt + 2 * HALO, dc), F32), pltpu.VMEM((tt, dc), F32)],
        compiler_params=_cparams(("parallel", "parallel")),
        name="conv",
    )(u3, u3, u3, gates, wdw, bdw, lng, lnb, wc_bf, bc)


def _attn_kernel(lam_ref, q_ref, kl_ref, vl_ref, kc_ref, vc_ref, gs_ref, o_ref, vext, s_scr, m_scr,
                 *, tq, kb, nq, n_tiles, out_scale):
    n = pl.program_id(0)
    t = kl_ref.shape[1]
    pair = lax.div(n, nq)
    prev_pair = lax.div(jnp.maximum(n - 1, 0), nq)

    @pl.when((lax.rem(n, nq) == 0) & (n < n_tiles))
    def _():
        slot = lax.rem(pair, 2)
        vext[slot, 0:V_DIM, 0:t] = vl_ref[0].astype(F32).T.astype(vext.dtype)
        vext[slot, 0:V_DIM, t:] = vc_ref[0].astype(F32).T.astype(vext.dtype)
        row = lax.broadcasted_iota(jnp.int32, (vext.shape[1] - V_DIM, vext.shape[2]), 0)
        vext[slot, V_DIM:, :] = jnp.where(row == 0, 1.0, 0.0).astype(vext.dtype)

    @pl.when(n == 0)
    def _():
        s_scr[...] = jnp.zeros_like(s_scr)
        m_scr[...] = jnp.zeros_like(m_scr)

    q = q_ref[0]
    lane = lax.broadcasted_iota(jnp.int32, q.shape, 1)
    zero = jnp.zeros_like(q)
    qs = jnp.concatenate([jnp.where(lane < HEAD_DIM, q, zero), jnp.where(lane >= HEAD_DIM, q, zero)], axis=0)
    qs_t = qs.astype(F32).T.astype(BF16)
    vprev = vext.at[lax.rem(prev_pair, 2)]
    m_old = m_scr[...]
    acc = None
    m_new = None
    blocks = [(kl_ref, r, kb) for r in range(0, t, kb)] + [(kc_ref, 0, kc_ref.shape[1])]
    row = 0
    for k_ref, r0, size in blocks:
        rows = slice(row, row + size)
        p = jnp.exp2(s_scr[rows, :] - m_old).astype(BF16)
        part = jnp.dot(vprev[:, rows], p, preferred_element_type=F32)
        acc = part if acc is None else acc + part
        s_blk = jnp.dot(k_ref[0, r0:r0 + size, :], qs_t, preferred_element_type=F32)
        s_scr[rows, :] = s_blk
        blk_max = jnp.max(s_blk, axis=0, keepdims=True)
        m_new = blk_max if m_new is None else jnp.maximum(m_new, blk_max)
        row += size
    m_scr[...] = m_new

    o = acc[:V_DIM] / acc[V_DIM:V_DIM + 1]
    o = o[:, :tq] - lam_ref[0] * o[:, tq:]
    ms = jnp.mean(o * o, axis=0, keepdims=True)
    o = o * lax.rsqrt(ms + EPS) * gs_ref[...] * out_scale
    o_ref[...] = o.T.astype(o_ref.dtype)


def _attention(lam, qkv, kvc, g_subln, bsz, t, n_ctx, out_scale):
    m, width = qkv.shape[1], qkv.shape[2]
    nh = width // V_DIM
    tq = _tile(t, 512)
    nq = t // tq
    s_all = t + n_ctx
    n_tiles = bsz * nh * nq
    kern = functools.partial(_attn_kernel, tq=tq, kb=_tile(t, 256), nq=nq, n_tiles=n_tiles, out_scale=out_scale)

    def tile_coords(n):
        pair = lax.div(n, nq)
        return lax.div(pair, nh), lax.rem(pair, nh), lax.rem(n, nq)

    def started(n):
        return tile_coords(jnp.minimum(n, n_tiles - 1))

    def finished(n):
        return tile_coords(jnp.maximum(n - 1, 0))

    def q_map(n):
        b, h, j = started(n)
        return 0, b * nq + j, h

    def kv_map(which):
        def index(n):
            b, h, _ = started(n)
            return which, b, h
        return index

    def out_map(n):
        b, h, j = finished(n)
        return b * nq + j, h

    return pl.pallas_call(
        kern,
        grid=(n_tiles + 1,),
        in_specs=[pl.BlockSpec(memory_space=pltpu.SMEM),
                  pl.BlockSpec((1, tq, V_DIM), q_map),
                  pl.BlockSpec((1, t, V_DIM), kv_map(1)),
                  pl.BlockSpec((1, t, V_DIM), kv_map(2)),
                  pl.BlockSpec((1, n_ctx, V_DIM), kv_map(0)),
                  pl.BlockSpec((1, n_ctx, V_DIM), kv_map(1)),
                  pl.BlockSpec((V_DIM, 1), lambda n: (0, 0))],
        out_specs=pl.BlockSpec((tq, V_DIM), out_map),
        out_shape=jax.ShapeDtypeStruct((m, width), BF16),
        scratch_shapes=[pltpu.VMEM((2, V_DIM + 2 * SUBLANES, s_all), BF16),
                        pltpu.VMEM((s_all, 2 * tq), F32),
                        pltpu.VMEM((1, 2 * tq), F32)],
        compiler_params=_cparams(("arbitrary",)),
        name="attn",
    )(lam, qkv, qkv, qkv, kvc, kvc, g_subln.reshape(V_DIM, 1))


def _post_kernel(o_ref, ga_ref, gy_ref, x_ref, mod_ref, wa_ref, wo_ref, gf_ref, wrt_ref,
                 xmid_ref, hf_ref, aff_ref):
    y_attn = jnp.dot(o_ref[...], wa_ref[...], preferred_element_type=F32)
    merged = ga_ref[0].astype(F32) * y_attn + gy_ref[...].astype(F32)
    mix = jnp.dot(merged.astype(BF16), wo_ref[...], preferred_element_type=F32)
    xm = x_ref[...] + mod_ref[0, 2:3, :] * mix
    xmid_ref[...] = xm
    ms = jnp.mean(xm * xm, axis=-1, keepdims=True)
    hf = xm * lax.rsqrt(ms + EPS) * gf_ref[...]
    hf = hf * (1.0 + mod_ref[0, 4:5, :]) + mod_ref[0, 3:4, :]
    hf_ref[...] = hf.astype(hf_ref.dtype)
    logits = _dot3(wrt_ref[...], hf, _NT)
    z = jnp.exp(logits - jnp.max(logits, axis=0, keepdims=True))
    aff_ref[0] = z / jnp.sum(z, axis=0, keepdims=True)


def _post(o, gates, gy, x2, mod3, wa_bf, wo_bf, gf, wrt, bsz, t):
    m, d = x2.shape
    aw = o.shape[1]
    ne = wrt.shape[0]
    tm = _tile(t, 512)
    nt = t // tm
    const = lambda shape: pl.BlockSpec(shape, lambda i: (0,) * len(shape))
    return pl.pallas_call(
        _post_kernel,
        grid=(m // tm,),
        in_specs=[pl.BlockSpec((tm, aw), lambda i: (i, 0)),
                  pl.BlockSpec((1, tm, d), lambda i: (1, i, 0)),
                  pl.BlockSpec((tm, d), lambda i: (i, 0)),
                  pl.BlockSpec((tm, d), lambda i: (i, 0)),
                  pl.BlockSpec((1, N_MOD, d), lambda i: (i // nt, 0, 0)),
                  const((aw, d)), const((d, d)), const((1, d)), const((ne, d))],
        out_specs=[pl.BlockSpec((tm, d), lambda i: (i, 0)),
                   pl.BlockSpec((tm, d), lambda i: (i, 0)),
                   pl.BlockSpec((1, ne, tm), lambda i: (i // nt, 0, i % nt))],
        out_shape=[jax.ShapeDtypeStruct((m, d), F32),
                   jax.ShapeDtypeStruct((m, d), BF16),
                   jax.ShapeDtypeStruct((bsz, ne, t), F32)],
        compiler_params=_cparams(("parallel",)),
        name="post",
    )(o, gates, gy, x2, mod3, wa_bf, wo_bf, gf, wrt)


def _count(mask):
    ones = jnp.where(mask, 1.0, 0.0)
    return jnp.sum(jnp.sum(ones, axis=2, keepdims=True), axis=1, keepdims=True)


def _route_kernel(a_ref, pos_ref, rank_ref, *, cap):
    a = a_ref[0]
    ne, nc, ln = a.shape
    rows = ne * nc
    bits = lax.bitcast_convert_type(a, jnp.int32)

    def search(i, cur):
        cand = cur | jnp.left_shift(jnp.int32(1), 30 - i)
        return jnp.where(_count(bits >= cand) >= cap, cand, cur)

    tau = lax.fori_loop(0, 31, search, jnp.zeros((ne, 1, 1), jnp.int32))

    r_i = lax.broadcasted_iota(jnp.int32, (rows, rows), 0)
    r_j = lax.broadcasted_iota(jnp.int32, (rows, rows), 1)
    shift = int(math.log2(nc))
    same_expert = lax.shift_right_logical(r_i, shift) == lax.shift_right_logical(r_j, shift)
    chunk_before = jnp.where(same_expert & (r_j < r_i), 1.0, 0.0).astype(BF16)
    l_i = lax.broadcasted_iota(jnp.int32, (ln, ln), 0)
    l_j = lax.broadcasted_iota(jnp.int32, (ln, ln), 1)
    lane_upto = jnp.where(l_i <= l_j, 1.0, 0.0).astype(BF16)

    def excl_prefix(mask):
        xm = jnp.where(mask, 1.0, 0.0).reshape(rows, ln)
        incl = jnp.dot(xm.astype(BF16), lane_upto, preferred_element_type=F32)
        tot = jnp.broadcast_to(incl[:, ln - 1:ln], (rows, ln))
        offs = jnp.dot(chunk_before, tot.astype(BF16), preferred_element_type=F32)
        return (incl + offs - xm).reshape(ne, nc, ln)

    gt = bits > tau
    eq = bits == tau
    need = cap - _count(gt)
    sel = gt | (eq & (excl_prefix(eq) < need))
    pos = excl_prefix(sel)
    pos_ref[0] = jnp.where(sel, pos, -1.0).astype(jnp.int32)
    rank_ref[0] = pos.astype(jnp.int32)


def _route(aff4, cap):
    bsz, ne, nc, ln = aff4.shape
    assert nc & (nc - 1) == 0, "token chunks per sample must be a power of two"
    spec = pl.BlockSpec((1, ne, nc, ln), lambda b: (b, 0, 0, 0))
    return pl.pallas_call(
        functools.partial(_route_kernel, cap=cap),
        grid=(bsz,),
        in_specs=[spec],
        out_specs=[spec, spec],
        out_shape=[jax.ShapeDtypeStruct((bsz, ne, nc, ln), jnp.int32)] * 2,
        compiler_params=_cparams(("parallel",)),
        name="route",
    )(aff4)


def _slot_range(cs_ref, b, e, ne, nchunk, c):
    base = (b * ne + e) * (nchunk + 1) + c
    lo, hi = cs_ref[base], cs_ref[base + 1]
    return lax.shift_right_logical(lo, 4) * 16, hi


def _max_windows(starts_and_ends, win):
    n = None
    for start, end in starts_and_ends:
        n_e = lax.div(end - start + (win - 1), win)
        n = n_e if n is None else jnp.maximum(n, n_e)
    return n


def _gather_kernel(cs_ref, pos_ref, aff_ref, hf_ref, xs_ref, gate_ref, acc, gacc, *, cap, tc, win):
    b, eg = pl.program_id(0), pl.program_id(1)
    group = pos_ref.shape[1]
    ne = pl.num_programs(1) * group
    nchunk = hf_ref.shape[1] // tc
    acc[...] = jnp.zeros_like(acc)
    gacc[...] = jnp.zeros_like(gacc)
    for c in range(nchunk):
        ranges = [_slot_range(cs_ref, b, eg * group + g, ne, nchunk, c) for g in range(group)]
        cols = slice(c * tc, (c + 1) * tc)

        def window(w, carry):
            rows = [pl.multiple_of(jnp.minimum(r0 + w * win, cap), 16) for r0, _ in ranges]
            hits = [pos_ref[0, g, :, cols] == r + lax.broadcasted_iota(jnp.int32, (win, tc), 0)
                    for g, r in enumerate(rows)]
            onehot = jnp.concatenate([jnp.where(h, 1.0, 0.0).astype(BF16) for h in hits], axis=0)
            part = jnp.dot(onehot, hf_ref[0, cols, :], preferred_element_type=F32)
            for g, r in enumerate(rows):
                acc[g, pl.ds(r, win), :] += part[g * win:(g + 1) * win]
                gacc[g, pl.ds(r, win), :] += jnp.sum(jnp.where(hits[g], aff_ref[0, g, :, cols], 0.0),
                                                     axis=1, keepdims=True)
            return carry

        lax.fori_loop(0, _max_windows(ranges, win), window, 0)
    for g in range(group):
        xs_ref[g, 0] = acc[g, 0:cap, :].astype(xs_ref.dtype)
        gate_ref[g, 0] = gacc[g, 0:cap, :]


def _gather(chunk_slots, pos4, aff4, hf3, cap, tc):
    bsz, ne, _, t = pos4.shape
    d = hf3.shape[2]
    win = _tile(cap, 64)
    group = _tile(ne, 8)
    kern = functools.partial(_gather_kernel, cap=cap, tc=tc, win=win)
    return pl.pallas_call(
        kern,
        grid_spec=pltpu.PrefetchScalarGridSpec(
            num_scalar_prefetch=1,
            grid=(bsz, ne // group),
            in_specs=[pl.BlockSpec((1, group, 1, t), lambda b, e, cs: (b, e, 0, 0)),
                      pl.BlockSpec((1, group, 1, t), lambda b, e, cs: (b, e, 0, 0)),
                      pl.BlockSpec((1, t, d), lambda b, e, cs: (b, 0, 0), pipeline_mode=pl.Buffered(1))],
            out_specs=[pl.BlockSpec((group, 1, cap, d), lambda b, e, cs: (e, b, 0, 0)),
                       pl.BlockSpec((group, 1, cap, 1), lambda b, e, cs: (e, b, 0, 0))],
            scratch_shapes=[pltpu.VMEM((group, cap + win, d), F32), pltpu.VMEM((group, cap + win, 1), F32)]),
        out_shape=[jax.ShapeDtypeStruct((ne, bsz, cap, d), BF16),
                   jax.ShapeDtypeStruct((ne, bsz, cap, 1), F32)],
        compiler_params=_cparams(("parallel", "arbitrary")),
        name="gather",
    )(chunk_slots, pos4, aff4, hf3)


def _expert_kernel(xs_ref, gate_ref, wg_ref, wu_ref, wd_ref, o_ref, acc):
    f = pl.program_id(2)

    @pl.when(f == 0)
    def _():
        acc[...] = jnp.zeros_like(acc)

    x = xs_ref[0]
    g = jnp.dot(x, wg_ref[0].astype(BF16), preferred_element_type=F32)
    u = jnp.dot(x, wu_ref[0].astype(BF16), preferred_element_type=F32)
    hid = (g * _sigmoid(g)) * u
    acc[...] += jnp.dot(hid.astype(BF16), wd_ref[0].astype(BF16), preferred_element_type=F32)

    @pl.when(f == pl.num_programs(2) - 1)
    def _():
        o_ref[0] = (acc[...] * gate_ref[0]).astype(o_ref.dtype)


def _experts(xs3, gate3, wg, wu, wd):
    ne, r, d = xs3.shape
    ff = wg.shape[2]
    tr = _tile(r, 2048)
    tf = _tile(ff, 256)
    return pl.pallas_call(
        _expert_kernel,
        grid=(ne, r // tr, ff // tf),
        in_specs=[pl.BlockSpec((1, tr, d), lambda e, i, f: (e, i, 0)),
                  pl.BlockSpec((1, tr, 1), lambda e, i, f: (e, i, 0)),
                  pl.BlockSpec((1, d, tf), lambda e, i, f: (e, 0, f)),
                  pl.BlockSpec((1, d, tf), lambda e, i, f: (e, 0, f)),
                  pl.BlockSpec((1, tf, d), lambda e, i, f: (e, f, 0))],
        out_specs=pl.BlockSpec((1, tr, d), lambda e, i, f: (e, i, 0)),
        out_shape=jax.ShapeDtypeStruct((ne, r, d), BF16),
        scratch_shapes=[pltpu.VMEM((tr, d), F32)],
        compiler_params=_cparams(("parallel", "parallel", "arbitrary")),
        name="experts",
    )(xs3, gate3, wg, wu, wd)


def _combine_kernel(cs_ref, pos_ref, ys_ref, xm_ref, mod_ref, gfin_ref, o_ref, acc, pos_t, *, cap, win, group):
    b, i = pl.program_id(0), pl.program_id(1)
    nchunk = pl.num_programs(1)
    ne, tt = pos_ref.shape[1], pos_ref.shape[2]
    p = pos_ref[0].astype(F32)
    padded = jnp.concatenate([p, jnp.full((LANES - ne, tt), -1.0, F32)], axis=0)
    pos_t[...] = padded.T
    acc[...] = jnp.zeros_like(acc)
    for e0 in range(0, ne, group):
        experts = range(e0, e0 + group)
        ranges = [_slot_range(cs_ref, b, e, ne, nchunk, i) for e in experts]

        def window(w, carry):
            onehots, outputs = [], []
            for e, (k0, _) in zip(experts, ranges):
                first = k0 + w * win
                k = pl.multiple_of(jnp.minimum(first, cap - win), 16)
                slot = k + lax.broadcasted_iota(jnp.int32, (tt, win), 1)
                mine = (pos_t[:, e:e + 1] == slot.astype(F32)) & (slot >= first)
                onehots.append(jnp.where(mine, 1.0, 0.0).astype(BF16))
                outputs.append(ys_ref[e, 0, pl.ds(k, win), :])
            acc[...] += jnp.dot(jnp.concatenate(onehots, axis=1), jnp.concatenate(outputs, axis=0),
                                preferred_element_type=F32)
            return carry

        lax.fori_loop(0, _max_windows(ranges, win), window, 0)
    xo = xm_ref[...] + mod_ref[0, 5:6, :] * acc[...]
    ms = jnp.mean(xo * xo, axis=-1, keepdims=True)
    o_ref[...] = xo * lax.rsqrt(ms + EPS) * gfin_ref[...]


def _combine(chunk_slots, pos3, ys4, xmid, mod3, gfin, cap, tc):
    bsz, ne, t = pos3.shape
    m, d = xmid.shape
    nt = t // tc
    return pl.pallas_call(
        functools.partial(_combine_kernel, cap=cap, win=_tile(cap, 128), group=_tile(ne, 8)),
        grid_spec=pltpu.PrefetchScalarGridSpec(
            num_scalar_prefetch=1,
            grid=(bsz, nt),
            in_specs=[pl.BlockSpec((1, ne, tc), lambda b, i, cs: (b, 0, i)),
                      pl.BlockSpec((ne, 1, cap, d), lambda b, i, cs: (0, b, 0, 0), pipeline_mode=pl.Buffered(1)),
                      pl.BlockSpec((tc, d), lambda b, i, cs: (b * nt + i, 0)),
                      pl.BlockSpec((1, N_MOD, d), lambda b, i, cs: (b, 0, 0)),
                      pl.BlockSpec((1, d), lambda b, i, cs: (0, 0))],
            out_specs=pl.BlockSpec((tc, d), lambda b, i, cs: (b * nt + i, 0)),
            scratch_shapes=[pltpu.VMEM((tc, d), F32), pltpu.VMEM((tc, LANES), F32)]),
        out_shape=jax.ShapeDtypeStruct((m, d), F32),
        compiler_params=_cparams(("parallel", "arbitrary")),
        name="combine",
    )(chunk_slots, pos3, ys4, xmid, mod3, gfin)


def _rope_tables(t):
    n_freq = HEAD_DIM // 4
    tok = jnp.arange(t)
    inv = ROPE_THETA ** (-jnp.arange(n_freq, dtype=F32) / n_freq)
    ang_r = (tok // GRID_W)[:, None].astype(F32) * inv
    ang_c = (tok % GRID_W)[:, None].astype(F32) * inv
    cos64 = jnp.concatenate([jnp.cos(ang_r)] * 2 + [jnp.cos(ang_c)] * 2, axis=-1)
    sin64 = jnp.concatenate([-jnp.sin(ang_r), jnp.sin(ang_r), -jnp.sin(ang_c), jnp.sin(ang_c)], axis=-1)
    return jnp.tile(cos64, (1, 2)), jnp.tile(sin64, (1, 2))


def kernel(x, c, ctx, c_ctx, w_ada, b_ada, g_norm_mix, g_norm_ffn, w_in, w_dw, b_dw, ln_g_conv, ln_b_conv,
           w_conv_out, b_conv_out, lambda_q1, lambda_k1, lambda_q2, lambda_k2, g_subln, w_attn_out, w_out,
           w_router, w_expert_gate, w_expert_up, w_expert_down, g_final):
    bsz, t, d = x.shape
    n_ctx = ctx.shape[1]
    depth = w_ada.shape[0]
    assert depth == 1, "single-layer trunk: the context stream only feeds keys/values"
    ne = w_router.shape[2]
    cap = CAPACITY_FACTOR * t // ne
    dc = w_dw.shape[2]
    qkw = N_HEADS * 2 * HEAD_DIM
    aw = N_HEADS * V_DIM
    assert dc == d and qkw == d and aw == d, "projection groups are addressed as equal-width column blocks"
    layer = 0
    lam_init = 0.8 - 0.6 * math.exp(-0.3 * layer)
    lam = (jnp.exp(jnp.sum(lambda_q1[layer].astype(F32) * lambda_k1[layer].astype(F32)))
           - jnp.exp(jnp.sum(lambda_q2[layer].astype(F32) * lambda_k2[layer].astype(F32)))
           + lam_init).reshape(1)

    row = lambda v: v.reshape(1, -1)
    x2 = x.reshape(bsz * t, d)
    ctx2 = ctx.reshape(bsz * n_ctx, d)

    pad = (-(bsz + 1)) % 8
    cc = jnp.concatenate([c, c_ctx[None, :], jnp.zeros((pad, d), F32)], axis=0)
    mod = _adaln(cc, w_ada[layer], row(b_ada[layer]))
    mod_lat = mod[:bsz].reshape(bsz, N_MOD, d)
    mod_ctx = mod[bsz:bsz + 1].reshape(1, N_MOD, d)

    w_in_bf = w_in[layer].astype(BF16)
    g_mix = row(g_norm_mix[layer])
    cos, sin_signed = _rope_tables(t)

    u, qkv, gates = _proj_all(x2, mod_lat, g_mix, w_in_bf, cos, sin_signed, t)
    kvc = _proj_ctx(ctx2, mod_ctx, g_mix, w_in_bf, 3)

    gy = _conv_branch(u.reshape(bsz, t, dc), gates, w_dw[layer], row(b_dw[layer]), row(ln_g_conv[layer]),
                      row(ln_b_conv[layer]), w_conv_out[layer].astype(BF16), row(b_conv_out[layer]))
    o = _attention(lam, qkv, kvc, row(g_subln[layer]), bsz, t, n_ctx, 1.0 - lam_init)

    xmid, hf, aff_t = _post(o, gates, gy, x2, mod_lat, w_attn_out[layer].astype(BF16), w_out[layer].astype(BF16),
                            row(g_norm_ffn[layer]), w_router[layer].T, bsz, t)

    pos, rank = _route(aff_t.reshape(bsz, ne, t // LANES, LANES), cap)

    def chunk_slots(tc):
        return jnp.concatenate([rank[:, :, ::tc // LANES, 0], jnp.full((bsz, ne, 1), cap, jnp.int32)],
                               axis=-1).reshape(-1)

    tc_gather, tc_combine = _tile(t, 2 * LANES), _tile(t, 4 * LANES)
    xs, gate = _gather(chunk_slots(tc_gather), pos.reshape(bsz, ne, 1, t), aff_t.reshape(bsz, ne, 1, t),
                       hf.reshape(bsz, t, d), cap, tc_gather)
    ys = _experts(xs.reshape(ne, bsz * cap, d), gate.reshape(ne, bsz * cap, 1),
                  w_expert_gate[layer], w_expert_up[layer], w_expert_down[layer])
    out = _combine(chunk_slots(tc_combine), pos.reshape(bsz, ne, t), ys.reshape(ne, bsz, cap, d), xmid, mod_lat,
                   row(g_final), cap, tc_combine)
    return out.reshape(bsz, t, d)
```

```python
import functools
import math

import jax
import jax.numpy as jnp
from jax import lax
from jax.experimental import pallas as pl
from jax.experimental.pallas import tpu as pltpu

N_HEADS = 8
HEAD_DIM = 64
V_DIM = 2 * HEAD_DIM
CONV_WIDTH = 31
CONV_HALF = CONV_WIDTH // 2
N_MOD = 6
GRID_W = 64
ROPE_THETA = 10000.0
CAPACITY_FACTOR = 2
EPS = 1e-6

LANES = 128
SUBLANES = 8
HALO = 16
VMEM_LIMIT = 56 * 1024 * 1024

F32 = jnp.float32
BF16 = jnp.bfloat16


def _cparams(sem):
    return pltpu.CompilerParams(dimension_semantics=sem, vmem_limit_bytes=VMEM_LIMIT)


def _tile(n, pref):
    t = min(n, pref)
    while n % t:
        t //= 2
    return t


def _split_bf16(a):
    hi = a.astype(BF16)
    lo = (a - hi.astype(F32)).astype(BF16)
    return hi, lo


def _dot3(a, b, dims):
    ah, al = _split_bf16(a)
    bh, bl = _split_bf16(b)
    dg = functools.partial(lax.dot_general, dimension_numbers=dims, preferred_element_type=F32)
    return dg(ah, bh) + (dg(ah, bl) + dg(al, bh))


_NN = (((1,), (0,)), ((), ()))
_NT = (((1,), (1,)), ((), ()))


def _sigmoid(x):
    return 1.0 / (1.0 + jnp.exp(-x))


def _adaln_kernel(c_ref, w_ref, b_ref, o_ref):
    c = c_ref[...]
    s = c * _sigmoid(c)
    o_ref[...] = _dot3(s, w_ref[...], _NN) + b_ref[...]


def _adaln(cc, w, b):
    rows, d = cc.shape
    n = w.shape[1]
    tn = _tile(n, 1024)
    return pl.pallas_call(
        _adaln_kernel,
        grid=(n // tn,),
        in_specs=[pl.BlockSpec((rows, d), lambda j: (0, 0)),
                  pl.BlockSpec((d, tn), lambda j: (0, j)),
                  pl.BlockSpec((1, tn), lambda j: (0, j))],
        out_specs=pl.BlockSpec((rows, tn), lambda j: (0, j)),
        out_shape=jax.ShapeDtypeStruct((rows, n), F32),
        compiler_params=_cparams(("arbitrary",)),
        name="adaln",
    )(cc, w, b)


def _norm_mod(x_ref, mod_ref, g_ref):
    x = x_ref[...]
    ms = jnp.mean(x * x, axis=-1, keepdims=True)
    y = x * lax.rsqrt(ms + EPS) * g_ref[...]
    return (y * (1.0 + mod_ref[0, 1:2, :]) + mod_ref[0, 0:1, :]).astype(BF16)


def _rope(acc, cos, sin_signed):
    lane = lax.broadcasted_iota(jnp.int32, (acc.shape[0], LANES), 1)
    first_half = (lane % 32) < 16
    outs = []
    for hh in range(acc.shape[1] // LANES):
        seg = acc[:, hh * LANES:(hh + 1) * LANES]
        partner = jnp.where(first_half, pltpu.roll(seg, LANES - 16, 1), pltpu.roll(seg, 16, 1))
        outs.append(seg * cos + partner * sin_signed)
    return jnp.concatenate(outs, axis=1)


def _proj_ctx_kernel(x_ref, mod_ref, g_ref, wk_ref, wv_ref, o_ref):
    h = _norm_mod(x_ref, mod_ref, g_ref)
    o_ref[0] = jnp.dot(h, wk_ref[...], preferred_element_type=F32).astype(o_ref.dtype)
    o_ref[1] = jnp.dot(h, wv_ref[...], preferred_element_type=F32).astype(o_ref.dtype)


def _proj_ctx(x2, mod3, g, w_bf, k_group):
    m, d = x2.shape
    tm = _tile(m, 512)
    return pl.pallas_call(
        _proj_ctx_kernel,
        grid=(m // tm,),
        in_specs=[pl.BlockSpec((tm, d), lambda i: (i, 0)),
                  pl.BlockSpec((1, N_MOD, d), lambda i: (0, 0, 0)),
                  pl.BlockSpec((1, d), lambda i: (0, 0)),
                  pl.BlockSpec((d, d), lambda i: (0, k_group)),
                  pl.BlockSpec((d, d), lambda i: (0, k_group + 1))],
        out_specs=pl.BlockSpec((2, tm, d), lambda i: (0, i, 0)),
        out_shape=jax.ShapeDtypeStruct((2, m, d), BF16),
        compiler_params=_cparams(("parallel",)),
        name="proj_ctx",
    )(x2, mod3, g, w_bf, w_bf)


def _proj_all_kernel(x_ref, mod_ref, g_ref, w_ref, cos_ref, sin_ref, u_ref, qkv_ref, gate_ref):
    d = x_ref.shape[1]
    h = _norm_mod(x_ref, mod_ref, g_ref)

    def group(n):
        return jnp.dot(h, w_ref[:, n * d:(n + 1) * d], preferred_element_type=F32)

    cos = cos_ref[...]
    sin = sin_ref[...]
    u_ref[...] = (group(0) * _sigmoid(group(1))).astype(u_ref.dtype)
    qkv_ref[0] = _rope(group(2) * (HEAD_DIM ** -0.5 * math.log2(math.e)), cos, sin).astype(qkv_ref.dtype)
    qkv_ref[1] = _rope(group(3), cos, sin).astype(qkv_ref.dtype)
    qkv_ref[2] = group(4).astype(qkv_ref.dtype)
    gate_ref[0] = _sigmoid(group(5)).astype(gate_ref.dtype)
    gate_ref[1] = _sigmoid(group(6)).astype(gate_ref.dtype)


def _proj_all(x2, mod3, g, w_bf, cos, sin_signed, t):
    m, d = x2.shape
    tm = _tile(t, 512)
    tblocks = t // tm
    return pl.pallas_call(
        _proj_all_kernel,
        grid=(m // tm,),
        in_specs=[pl.BlockSpec((tm, d), lambda i: (i, 0)),
                  pl.BlockSpec((1, N_MOD, d), lambda i: (i // tblocks, 0, 0)),
                  pl.BlockSpec((1, d), lambda i: (0, 0)),
                  pl.BlockSpec(w_bf.shape, lambda i: (0, 0), pipeline_mode=pl.Buffered(1)),
                  pl.BlockSpec((tm, LANES), lambda i: (i % tblocks, 0)),
                  pl.BlockSpec((tm, LANES), lambda i: (i % tblocks, 0))],
        out_specs=[pl.BlockSpec((tm, d), lambda i: (i, 0)),
                   pl.BlockSpec((3, tm, d), lambda i: (0, i, 0)),
                   pl.BlockSpec((2, tm, d), lambda i: (0, i, 0))],
        out_shape=[jax.ShapeDtypeStruct((m, d), BF16),
                   jax.ShapeDtypeStruct((3, m, d), BF16),
                   jax.ShapeDtypeStruct((2, m, d), BF16)],
        compiler_params=_cparams(("parallel",)),
        name="proj_all",
    )(x2, mod3, g, w_bf, cos, sin_signed)


def _conv_kernel(up_ref, um_ref, un_ref, gc_ref, wdw_ref, bdw_ref, lng_ref, lnb_ref, wc_ref, bc_ref,
                 o_ref, buf, cv, *, tt, rb):
    ti = pl.program_id(1)
    nt = pl.num_programs(1)
    d = um_ref.shape[2]
    prev = up_ref[0].astype(F32)
    nxt = un_ref[0].astype(F32)
    buf[0:HALO, :] = jnp.where(ti > 0, prev, 0.0)
    buf[HALO:HALO + tt, :] = um_ref[0].astype(F32)
    buf[HALO + tt:2 * HALO + tt, :] = jnp.where(ti < nt - 1, nxt, 0.0)

    off = HALO - CONV_HALF
    taps = [[(s - off, s // SUBLANES) for s in range(off, off + CONV_WIDTH) if s % SUBLANES == r]
            for r in range(SUBLANES)]
    n_slab = (off + CONV_WIDTH - 1) // SUBLANES + 1
    for c in range(d // LANES):
        cs = slice(c * LANES, (c + 1) * LANES)
        for rblk in range(tt // rb):
            r0 = rblk * rb
            slabs = [buf[r0 + SUBLANES * a:r0 + SUBLANES * a + rb + SUBLANES, cs] for a in range(n_slab)]
            acc = None
            for r in range(SUBLANES):
                z = None
                for k, a in taps[r]:
                    term = slabs[a] * wdw_ref[k:k + 1, cs]
                    z = term if z is None else z + term
                z = z[r:r + rb]
                acc = z if acc is None else acc + z
            cv[r0:r0 + rb, cs] = acc

    v = cv[...] + bdw_ref[...]
    mu = jnp.mean(v, axis=-1, keepdims=True)
    xc = v - mu
    var = jnp.mean(xc * xc, axis=-1, keepdims=True)
    y = xc * lax.rsqrt(var + EPS) * lng_ref[...] + lnb_ref[...]
    z = y * _sigmoid(y)
    yc = jnp.dot(z.astype(BF16), wc_ref[...], preferred_element_type=F32) + bc_ref[...]
    o_ref[...] = (gc_ref[0].astype(F32) * yc).astype(o_ref.dtype)


def _conv_branch(u3, gates, wdw, bdw, lng, lnb, wc_bf, bc):
    bsz, t, dc = u3.shape
    d = wc_bf.shape[1]
    tt = _tile(t, 512)
    nt = t // tt
    hb = tt // HALO
    nhb = t // HALO
    kern = functools.partial(_conv_kernel, tt=tt, rb=_tile(tt, 128))
    const = lambda shape: pl.BlockSpec(shape, lambda b, i: (0,) * len(shape))
    return pl.pallas_call(
        kern,
        grid=(bsz, nt),
        in_specs=[pl.BlockSpec((1, HALO, dc), lambda b, i: (b, jnp.maximum(i * hb - 1, 0), 0)),
                  pl.BlockSpec((1, tt, dc), lambda b, i: (b, i, 0)),
                  pl.BlockSpec((1, HALO, dc), lambda b, i: (b, jnp.minimum((i + 1) * hb, nhb - 1), 0)),
                  pl.BlockSpec((1, tt, d), lambda b, i: (0, b * nt + i, 0)),
                  const((CONV_WIDTH, dc)), const((1, dc)), const((1, dc)), const((1, dc)),
                  const((dc, d)), const((1, d))],
        out_specs=pl.BlockSpec((tt, d), lambda b, i: (b * nt + i, 0)),
        out_shape=jax.ShapeDtypeStruct((bsz * t, d), BF16),
        scratch_shapes=[pltpu.VMEM((tt + 2 * HALO, dc), F32), pltpu.VMEM((tt, dc), F32)],
        compiler_params=_cparams(("parallel", "parallel")),
        name="conv",
    )(u3, u3, u3, gates, wdw, bdw, lng, lnb, wc_bf, bc)


def _attn_kernel(lam_ref, q_ref, kl_ref, vl_ref, kc_ref, vc_ref, gs_ref, o_ref, vext, s_scr, m_scr,
                 *, tq, kb, nq, n_tiles, out_scale):
    n = pl.program_id(0)
    t = kl_ref.shape[1]
    pair = lax.div(n, nq)
    prev_pair = lax.div(jnp.maximum(n - 1, 0), nq)

    @pl.when((lax.rem(n, nq) == 0) & (n < n_tiles))
    def _():
        slot = lax.rem(pair, 2)
        vext[slot, 0:V_DIM, 0:t] = vl_ref[0].astype(F32).T.astype(vext.dtype)
        vext[slot, 0:V_DIM, t:] = vc_ref[0].astype(F32).T.astype(vext.dtype)
        row = lax.broadcasted_iota(jnp.int32, (vext.shape[1] - V_DIM, vext.shape[2]), 0)
        vext[slot, V_DIM:, :] = jnp.where(row == 0, 1.0, 0.0).astype(vext.dtype)

    @pl.when(n == 0)
    def _():
        s_scr[...] = jnp.zeros_like(s_scr)
        m_scr[...] = jnp.zeros_like(m_scr)

    q = q_ref[0]
    lane = lax.broadcasted_iota(jnp.int32, q.shape, 1)
    zero = jnp.zeros_like(q)
    qs = jnp.concatenate([jnp.where(lane < HEAD_DIM, q, zero), jnp.where(lane >= HEAD_DIM, q, zero)], axis=0)
    qs_t = qs.astype(F32).T.astype(BF16)
    vprev = vext.at[lax.rem(prev_pair, 2)]
    m_old = m_scr[...]
    acc = None
    m_new = None
    blocks = [(kl_ref, r, kb) for r in range(0, t, kb)] + [(kc_ref, 0, kc_ref.shape[1])]
    row = 0
    for k_ref, r0, size in blocks:
        rows = slice(row, row + size)
        p = jnp.exp2(s_scr[rows, :] - m_old).astype(BF16)
        part = jnp.dot(vprev[:, rows], p, preferred_element_type=F32)
        acc = part if acc is None else acc + part
        s_blk = jnp.dot(k_ref[0, r0:r0 + size, :], qs_t, preferred_element_type=F32)
        s_scr[rows, :] = s_blk
        blk_max = jnp.max(s_blk, axis=0, keepdims=True)
        m_new = blk_max if m_new is None else jnp.maximum(m_new, blk_max)
        row += size
    m_scr[...] = m_new

    o = acc[:V_DIM] / acc[V_DIM:V_DIM + 1]
    o = o[:, :tq] - lam_ref[0] * o[:, tq:]
    ms = jnp.mean(o * o, axis=0, keepdims=True)
    o = o * lax.rsqrt(ms + EPS) * gs_ref[...] * out_scale
    o_ref[...] = o.T.astype(o_ref.dtype)


def _attention(lam, qkv, kvc, g_subln, bsz, t, n_ctx, out_scale):
    m, width = qkv.shape[1], qkv.shape[2]
    nh = width // V_DIM
    tq = _tile(t, 512)
    nq = t // tq
    s_all = t + n_ctx
    n_tiles = bsz * nh * nq
    kern = functools.partial(_attn_kernel, tq=tq, kb=_tile(t, 256), nq=nq, n_tiles=n_tiles, out_scale=out_scale)

    def tile_coords(n):
        pair = lax.div(n, nq)
        return lax.div(pair, nh), lax.rem(pair, nh), lax.rem(n, nq)

    def started(n):
        return tile_coords(jnp.minimum(n, n_tiles - 1))

    def finished(n):
        return tile_coords(jnp.maximum(n - 1, 0))

    def q_map(n):
        b, h, j = started(n)
        return 0, b * nq + j, h

    def kv_map(which):
        def index(n):
            b, h, _ = started(n)
            return which, b, h
        return index

    def out_map(n):
        b, h, j = finished(n)
        return b * nq + j, h

    return pl.pallas_call(
        kern,
        grid=(n_tiles + 1,),
        in_specs=[pl.BlockSpec(memory_space=pltpu.SMEM),
                  pl.BlockSpec((1, tq, V_DIM), q_map),
                  pl.BlockSpec((1, t, V_DIM), kv_map(1)),
                  pl.BlockSpec((1, t, V_DIM), kv_map(2)),
                  pl.BlockSpec((1, n_ctx, V_DIM), kv_map(0)),
                  pl.BlockSpec((1, n_ctx, V_DIM), kv_map(1)),
                  pl.BlockSpec((V_DIM, 1), lambda n: (0, 0))],
        out_specs=pl.BlockSpec((tq, V_DIM), out_map),
        out_shape=jax.ShapeDtypeStruct((m, width), BF16),
        scratch_shapes=[pltpu.VMEM((2, V_DIM + 2 * SUBLANES, s_all), BF16),
                        pltpu.VMEM((s_all, 2 * tq), F32),
                        pltpu.VMEM((1, 2 * tq), F32)],
        compiler_params=_cparams(("arbitrary",)),
        name="attn",
    )(lam, qkv, qkv, qkv, kvc, kvc, g_subln.reshape(V_DIM, 1))


def _post_kernel(o_ref, ga_ref, gy_ref, x_ref, mod_ref, wa_ref, wo_ref, gf_ref, wrt_ref,
                 xmid_ref, hf_ref, aff_ref):
    y_attn = jnp.dot(o_ref[...], wa_ref[...], preferred_element_type=F32)
    merged = ga_ref[0].astype(F32) * y_attn + gy_ref[...].astype(F32)
    mix = jnp.dot(merged.astype(BF16), wo_ref[...], preferred_element_type=F32)
    xm = x_ref[...] + mod_ref[0, 2:3, :] * mix
    xmid_ref[...] = xm
    ms = jnp.mean(xm * xm, axis=-1, keepdims=True)
    hf = xm * lax.rsqrt(ms + EPS) * gf_ref[...]
    hf = hf * (1.0 + mod_ref[0, 4:5, :]) + mod_ref[0, 3:4, :]
    hf_ref[...] = hf.astype(hf_ref.dtype)
    logits = _dot3(wrt_ref[...], hf, _NT)
    z = jnp.exp(logits - jnp.max(logits, axis=0, keepdims=True))
    aff_ref[0] = z / jnp.sum(z, axis=0, keepdims=True)


def _post(o, gates, gy, x2, mod3, wa_bf, wo_bf, gf, wrt, bsz, t):
    m, d = x2.shape
    aw = o.shape[1]
    ne = wrt.shape[0]
    tm = _tile(t, 1024)
    nt = t // tm
    const = lambda shape: pl.BlockSpec(shape, lambda i: (0,) * len(shape))
    return pl.pallas_call(
        _post_kernel,
        grid=(m // tm,),
        in_specs=[pl.BlockSpec((tm, aw), lambda i: (i, 0)),
                  pl.BlockSpec((1, tm, d), lambda i: (1, i, 0)),
                  pl.BlockSpec((tm, d), lambda i: (i, 0)),
                  pl.BlockSpec((tm, d), lambda i: (i, 0)),
                  pl.BlockSpec((1, N_MOD, d), lambda i: (i // nt, 0, 0)),
                  const((aw, d)), const((d, d)), const((1, d)), const((ne, d))],
        out_specs=[pl.BlockSpec((tm, d), lambda i: (i, 0)),
                   pl.BlockSpec((tm, d), lambda i: (i, 0)),
                   pl.BlockSpec((1, ne, tm), lambda i: (i // nt, 0, i % nt))],
        out_shape=[jax.ShapeDtypeStruct((m, d), F32),
                   jax.ShapeDtypeStruct((m, d), BF16),
                   jax.ShapeDtypeStruct((bsz, ne, t), F32)],
        compiler_params=_cparams(("parallel",)),
        name="post",
    )(o, gates, gy, x2, mod3, wa_bf, wo_bf, gf, wrt)


def _count(mask):
    ones = jnp.where(mask, 1.0, 0.0)
    return jnp.sum(jnp.sum(ones, axis=2, keepdims=True), axis=1, keepdims=True)


def _route_kernel(a_ref, pos_ref, rank_ref, *, cap):
    a = a_ref[0]
    ne, nc, ln = a.shape
    rows = ne * nc
    bits = lax.bitcast_convert_type(a, jnp.int32)

    def search(i, cur):
        cand = cur | jnp.left_shift(jnp.int32(1), 30 - i)
        return jnp.where(_count(bits >= cand) >= cap, cand, cur)

    tau = lax.fori_loop(0, 31, search, jnp.zeros((ne, 1, 1), jnp.int32))

    r_i = lax.broadcasted_iota(jnp.int32, (rows, rows), 0)
    r_j = lax.broadcasted_iota(jnp.int32, (rows, rows), 1)
    shift = int(math.log2(nc))
    same_expert = lax.shift_right_logical(r_i, shift) == lax.shift_right_logical(r_j, shift)
    chunk_before = jnp.where(same_expert & (r_j < r_i), 1.0, 0.0).astype(BF16)
    l_i = lax.broadcasted_iota(jnp.int32, (ln, ln), 0)
    l_j = lax.broadcasted_iota(jnp.int32, (ln, ln), 1)
    lane_upto = jnp.where(l_i <= l_j, 1.0, 0.0).astype(BF16)

    def excl_prefix(mask):
        xm = jnp.where(mask, 1.0, 0.0).reshape(rows, ln)
        incl = jnp.dot(xm.astype(BF16), lane_upto, preferred_element_type=F32)
        tot = jnp.broadcast_to(incl[:, ln - 1:ln], (rows, ln))
        offs = jnp.dot(chunk_before, tot.astype(BF16), preferred_element_type=F32)
        return (incl + offs - xm).reshape(ne, nc, ln)

    gt = bits > tau
    eq = bits == tau
    need = cap - _count(gt)
    sel = gt | (eq & (excl_prefix(eq) < need))
    pos = excl_prefix(sel)
    pos_ref[0] = jnp.where(sel, pos, -1.0).astype(jnp.int32)
    rank_ref[0] = pos.astype(jnp.int32)


def _route(aff4, cap):
    bsz, ne, nc, ln = aff4.shape
    assert nc & (nc - 1) == 0, "token chunks per sample must be a power of two"
    spec = pl.BlockSpec((1, ne, nc, ln), lambda b: (b, 0, 0, 0))
    return pl.pallas_call(
        functools.partial(_route_kernel, cap=cap),
        grid=(bsz,),
        in_specs=[spec],
        out_specs=[spec, spec],
        out_shape=[jax.ShapeDtypeStruct((bsz, ne, nc, ln), jnp.int32)] * 2,
        compiler_params=_cparams(("parallel",)),
        name="route",
    )(aff4)


def _slot_range(cs_ref, b, e, ne, nchunk, c):
    base = (b * ne + e) * (nchunk + 1) + c
    lo, hi = cs_ref[base], cs_ref[base + 1]
    return lax.shift_right_logical(lo, 4) * 16, hi


def _max_windows(starts_and_ends, win):
    n = None
    for start, end in starts_and_ends:
        n_e = lax.div(end - start + (win - 1), win)
        n = n_e if n is None else jnp.maximum(n, n_e)
    return n


def _gather_kernel(cs_ref, pos_ref, aff_ref, hf_ref, xs_ref, gate_ref, acc, gacc, *, cap, tc, win):
    b, eg = pl.program_id(0), pl.program_id(1)
    group = pos_ref.shape[1]
    ne = pl.num_programs(1) * group
    nchunk = hf_ref.shape[1] // tc
    acc[...] = jnp.zeros_like(acc)
    gacc[...] = jnp.zeros_like(gacc)
    for c in range(nchunk):
        ranges = [_slot_range(cs_ref, b, eg * group + g, ne, nchunk, c) for g in range(group)]
        cols = slice(c * tc, (c + 1) * tc)

        def window(w, carry):
            rows = [pl.multiple_of(jnp.minimum(r0 + w * win, cap), 16) for r0, _ in ranges]
            hits = [pos_ref[0, g, :, cols] == r + lax.broadcasted_iota(jnp.int32, (win, tc), 0)
                    for g, r in enumerate(rows)]
            onehot = jnp.concatenate([jnp.where(h, 1.0, 0.0).astype(BF16) for h in hits], axis=0)
            part = jnp.dot(onehot, hf_ref[0, cols, :], preferred_element_type=F32)
            for g, r in enumerate(rows):
                acc[g, pl.ds(r, win), :] += part[g * win:(g + 1) * win]
                gacc[g, pl.ds(r, win), :] += jnp.sum(jnp.where(hits[g], aff_ref[0, g, :, cols], 0.0),
                                                     axis=1, keepdims=True)
            return carry

        lax.fori_loop(0, _max_windows(ranges, win), window, 0)
    for g in range(group):
        xs_ref[g, 0] = acc[g, 0:cap, :].astype(xs_ref.dtype)
        gate_ref[g, 0] = gacc[g, 0:cap, :]


def _gather(chunk_slots, pos4, aff4, hf3, cap, tc):
    bsz, ne, _, t = pos4.shape
    d = hf3.shape[2]
    win = _tile(cap, 64)
    group = _tile(ne, 8)
    kern = functools.partial(_gather_kernel, cap=cap, tc=tc, win=win)
    return pl.pallas_call(
        kern,
        grid_spec=pltpu.PrefetchScalarGridSpec(
            num_scalar_prefetch=1,
            grid=(bsz, ne // group),
            in_specs=[pl.BlockSpec((1, group, 1, t), lambda b, e, cs: (b, e, 0, 0)),
                      pl.BlockSpec((1, group, 1, t), lambda b, e, cs: (b, e, 0, 0)),
                      pl.BlockSpec((1, t, d), lambda b, e, cs: (b, 0, 0), pipeline_mode=pl.Buffered(1))],
            out_specs=[pl.BlockSpec((group, 1, cap, d), lambda b, e, cs: (e, b, 0, 0)),
                       pl.BlockSpec((group, 1, cap, 1), lambda b, e, cs: (e, b, 0, 0))],
            scratch_shapes=[pltpu.VMEM((group, cap + win, d), F32), pltpu.VMEM((group, cap + win, 1), F32)]),
        out_shape=[jax.ShapeDtypeStruct((ne, bsz, cap, d), BF16),
                   jax.ShapeDtypeStruct((ne, bsz, cap, 1), F32)],
        compiler_params=_cparams(("parallel", "arbitrary")),
        name="gather",
    )(chunk_slots, pos4, aff4, hf3)


def _expert_kernel(xs_ref, gate_ref, wg_ref, wu_ref, wd_ref, o_ref, acc):
    f = pl.program_id(2)

    @pl.when(f == 0)
    def _():
        acc[...] = jnp.zeros_like(acc)

    x = xs_ref[0]
    g = jnp.dot(x, wg_ref[0].astype(BF16), preferred_element_type=F32)
    u = jnp.dot(x, wu_ref[0].astype(BF16), preferred_element_type=F32)
    hid = (g * _sigmoid(g)) * u
    acc[...] += jnp.dot(hid.astype(BF16), wd_ref[0].astype(BF16), preferred_element_type=F32)

    @pl.when(f == pl.num_programs(2) - 1)
    def _():
        o_ref[0] = (acc[...] * gate_ref[0]).astype(o_ref.dtype)


def _experts(xs3, gate3, wg, wu, wd):
    ne, r, d = xs3.shape
    ff = wg.shape[2]
    tr = _tile(r, 2048)
    tf = _tile(ff, 256)
    return pl.pallas_call(
        _expert_kernel,
        grid=(ne, r // tr, ff // tf),
        in_specs=[pl.BlockSpec((1, tr, d), lambda e, i, f: (e, i, 0)),
                  pl.BlockSpec((1, tr, 1), lambda e, i, f: (e, i, 0)),
                  pl.BlockSpec((1, d, tf), lambda e, i, f: (e, 0, f)),
                  pl.BlockSpec((1, d, tf), lambda e, i, f: (e, 0, f)),
                  pl.BlockSpec((1, tf, d), lambda e, i, f: (e, f, 0))],
        out_specs=pl.BlockSpec((1, tr, d), lambda e, i, f: (e, i, 0)),
        out_shape=jax.ShapeDtypeStruct((ne, r, d), BF16),
        scratch_shapes=[pltpu.VMEM((tr, d), F32)],
        compiler_params=_cparams(("parallel", "parallel", "arbitrary")),
        name="experts",
    )(xs3, gate3, wg, wu, wd)


def _combine_kernel(cs_ref, pos_ref, ys_ref, xm_ref, mod_ref, gfin_ref, o_ref, acc, pos_t, *, cap, win, group):
    b, i = pl.program_id(0), pl.program_id(1)
    nchunk = pl.num_programs(1)
    ne, tt = pos_ref.shape[1], pos_ref.shape[2]
    p = pos_ref[0].astype(F32)
    padded = jnp.concatenate([p, jnp.full((LANES - ne, tt), -1.0, F32)], axis=0)
    pos_t[...] = padded.T
    acc[...] = jnp.zeros_like(acc)
    for e0 in range(0, ne, group):
        experts = range(e0, e0 + group)
        ranges = [_slot_range(cs_ref, b, e, ne, nchunk, i) for e in experts]

        def window(w, carry):
            onehots, outputs = [], []
            for e, (k0, _) in zip(experts, ranges):
                first = k0 + w * win
                k = pl.multiple_of(jnp.minimum(first, cap - win), 16)
                slot = k + lax.broadcasted_iota(jnp.int32, (tt, win), 1)
                mine = (pos_t[:, e:e + 1] == slot.astype(F32)) & (slot >= first)
                onehots.append(jnp.where(mine, 1.0, 0.0).astype(BF16))
                outputs.append(ys_ref[e, 0, pl.ds(k, win), :])
            acc[...] += jnp.dot(jnp.concatenate(onehots, axis=1), jnp.concatenate(outputs, axis=0),
                                preferred_element_type=F32)
            return carry

        lax.fori_loop(0, _max_windows(ranges, win), window, 0)
    xo = xm_ref[...] + mod_ref[0, 5:6, :] * acc[...]
    ms = jnp.mean(xo * xo, axis=-1, keepdims=True)
    o_ref[...] = xo * lax.rsqrt(ms + EPS) * gfin_ref[...]


def _combine(chunk_slots, pos3, ys4, xmid, mod3, gfin, cap, tc):
    bsz, ne, t = pos3.shape
    m, d = xmid.shape
    nt = t // tc
    return pl.pallas_call(
        functools.partial(_combine_kernel, cap=cap, win=_tile(cap, 128), group=_tile(ne, 8)),
        grid_spec=pltpu.PrefetchScalarGridSpec(
            num_scalar_prefetch=1,
            grid=(bsz, nt),
            in_specs=[pl.BlockSpec((1, ne, tc), lambda b, i, cs: (b, 0, i)),
                      pl.BlockSpec((ne, 1, cap, d), lambda b, i, cs: (0, b, 0, 0), pipeline_mode=pl.Buffered(1)),
                      pl.BlockSpec((tc, d), lambda b, i, cs: (b * nt + i, 0)),
                      pl.BlockSpec((1, N_MOD, d), lambda b, i, cs: (b, 0, 0)),
                      pl.BlockSpec((1, d), lambda b, i, cs: (0, 0))],
            out_specs=pl.BlockSpec((tc, d), lambda b, i, cs: (b * nt + i, 0)),
            scratch_shapes=[pltpu.VMEM((tc, d), F32), pltpu.VMEM((tc, LANES), F32)]),
        out_shape=jax.ShapeDtypeStruct((m, d), F32),
        compiler_params=_cparams(("parallel", "arbitrary")),
        name="combine",
    )(chunk_slots, pos3, ys4, xmid, mod3, gfin)


def _rope_tables(t):
    n_freq = HEAD_DIM // 4
    tok = jnp.arange(t)
    inv = ROPE_THETA ** (-jnp.arange(n_freq, dtype=F32) / n_freq)
    ang_r = (tok // GRID_W)[:, None].astype(F32) * inv
    ang_c = (tok % GRID_W)[:, None].astype(F32) * inv
    cos64 = jnp.concatenate([jnp.cos(ang_r)] * 2 + [jnp.cos(ang_c)] * 2, axis=-1)
    sin64 = jnp.concatenate([-jnp.sin(ang_r), jnp.sin(ang_r), -jnp.sin(ang_c), jnp.sin(ang_c)], axis=-1)
    return jnp.tile(cos64, (1, 2)), jnp.tile(sin64, (1, 2))


def kernel(x, c, ctx, c_ctx, w_ada, b_ada, g_norm_mix, g_norm_ffn, w_in, w_dw, b_dw, ln_g_conv, ln_b_conv,
           w_conv_out, b_conv_out, lambda_q1, lambda_k1, lambda_q2, lambda_k2, g_subln, w_attn_out, w_out,
           w_router, w_expert_gate, w_expert_up, w_expert_down, g_final):
    bsz, t, d = x.shape
    n_ctx = ctx.shape[1]
    depth = w_ada.shape[0]
    assert depth == 1, "single-layer trunk: the context stream only feeds keys/values"
    ne = w_router.shape[2]
    cap = CAPACITY_FACTOR * t // ne
    dc = w_dw.shape[2]
    qkw = N_HEADS * 2 * HEAD_DIM
    aw = N_HEADS * V_DIM
    assert dc == d and qkw == d and aw == d, "projection groups are addressed as equal-width column blocks"
    layer = 0
    lam_init = 0.8 - 0.6 * math.exp(-0.3 * layer)
    lam = (jnp.exp(jnp.sum(lambda_q1[layer].astype(F32) * lambda_k1[layer].astype(F32)))
           - jnp.exp(jnp.sum(lambda_q2[layer].astype(F32) * lambda_k2[layer].astype(F32)))
           + lam_init).reshape(1)

    row = lambda v: v.reshape(1, -1)
    x2 = x.reshape(bsz * t, d)
    ctx2 = ctx.reshape(bsz * n_ctx, d)

    pad = (-(bsz + 1)) % 8
    cc = jnp.concatenate([c, c_ctx[None, :], jnp.zeros((pad, d), F32)], axis=0)
    mod = _adaln(cc, w_ada[layer], row(b_ada[layer]))
    mod_lat = mod[:bsz].reshape(bsz, N_MOD, d)
    mod_ctx = mod[bsz:bsz + 1].reshape(1, N_MOD, d)

    w_in_bf = w_in[layer].astype(BF16)
    g_mix = row(g_norm_mix[layer])
    cos, sin_signed = _rope_tables(t)

    u, qkv, gates = _proj_all(x2, mod_lat, g_mix, w_in_bf, cos, sin_signed, t)
    kvc = _proj_ctx(ctx2, mod_ctx, g_mix, w_in_bf, 3)

    gy = _conv_branch(u.reshape(bsz, t, dc), gates, w_dw[layer], row(b_dw[layer]), row(ln_g_conv[layer]),
                      row(ln_b_conv[layer]), w_conv_out[layer].astype(BF16), row(b_conv_out[layer]))
    o = _attention(lam, qkv, kvc, row(g_subln[layer]), bsz, t, n_ctx, 1.0 - lam_init)

    xmid, hf, aff_t = _post(o, gates, gy, x2, mod_lat, w_attn_out[layer].astype(BF16), w_out[layer].astype(BF16),
                            row(g_norm_ffn[layer]), w_router[layer].T, bsz, t)

    pos, rank = _route(aff_t.reshape(bsz, ne, t // LANES, LANES), cap)

    def chunk_slots(tc):
        return jnp.concatenate([rank[:, :, ::tc // LANES, 0], jnp.full((bsz, ne, 1), cap, jnp.int32)],
                               axis=-1).reshape(-1)

    tc_gather, tc_combine = _tile(t, 2 * LANES), _tile(t, 4 * LANES)
    xs, gate = _gather(chunk_slots(tc_gather), pos.reshape(bsz, ne, 1, t), aff_t.reshape(bsz, ne, 1, t),
                       hf.reshape(bsz, t, d), cap, tc_gather)
    ys = _experts(xs.reshape(ne, bsz * cap, d), gate.reshape(ne, bsz * cap, 1),
                  w_expert_gate[layer], w_expert_up[layer], w_expert_down[layer])
    out = _combine(chunk_slots(tc_combine), pos.reshape(bsz, ne, t), ys.reshape(ne, bsz, cap, d), xmid, mod_lat,
                   row(g_final), cap, tc_combine)
    return out.reshape(bsz, t, d)
```
